```python
import math
import jax
import jax.numpy as jnp
from jax import lax
import numpy as np

D_MODEL = 2048
BATCH = 2
SEQ = 16384
DEPTH = 2
DEC_BATCH = 32
DEC_SEQ = 64
PAST_LEN = 4096

CHUNK = 64
EPS = 1e-6
GN_EPS = 64e-5
W_MIX = D_MODEL
W_GRP = W_MIX // 4
M_HEADS = 4
M_DH = W_GRP // M_HEADS
S5_CH = 16
S5_GROUPS = W_GRP // S5_CH
S5_P = 64
R_DH = 64
R_HEADS = W_GRP // R_DH
R_LR_W = 64
R_LR_A = 64
R_LR_G = 128
G_HEADS = 4
G_DH = W_GRP // G_HEADS
CONV_W = 4
D_FF = -(-8 * D_MODEL // (3 * 256)) * 256
N_M = 4 * W_GRP + 2 * M_HEADS
N_S = W_GRP
N_R = 3 * W_GRP + R_LR_W + R_LR_A + R_LR_G
N_G = 4 * W_GRP + 2 * G_HEADS
N_IN = N_M + N_S + N_R + N_G

kernel_name = 'hybrid_stream_mlstm_s5_rwkv7_gdn_step'


def rmsnorm(x, g):
    xf = x.astype(jnp.float32)
    y = xf * lax.rsqrt(jnp.mean(xf * xf, -1, keepdims=True) + EPS) * g.astype(jnp.float32)
    return y.astype(x.dtype)


def l2norm(t):
    return t * lax.rsqrt(jnp.sum(t * t, -1, keepdims=True) + 1e-6)


def to_chunks(a, L):
    B, T = a.shape[:2]
    return jnp.moveaxis(a.reshape((B, T // L, L) + a.shape[2:]), 1, 0)


def from_chunks(a):
    NC, B, L = a.shape[:3]
    return jnp.moveaxis(a, 0, 1).reshape((B, NC * L) + a.shape[3:])


def mlstm_chunkwise(q, k, v, i_pre, f_pre, C0, n0, m0):
    D = q.shape[-1]
    L = min(CHUNK, q.shape[1])
    tri = jnp.tril(jnp.ones((L, L), dtype=bool))
    q = q * (D ** -0.5)
    lf = jax.nn.log_sigmoid(f_pre)

    def step(carry, inp):
        C, n, m = carry
        qc, kc, vc, lic, lfc = inp
        b = jnp.cumsum(jnp.swapaxes(lfc, 1, 2), axis=-1)
        li = jnp.swapaxes(lic, 1, 2)
        logD = jnp.where(tri, b[..., :, None] - b[..., None, :] + li[..., None, :], -jnp.inf)
        inter = b + m[..., None]
        mt = jnp.maximum(inter, jnp.max(logD, axis=-1))
        s = jnp.einsum('bthd,bshd->bhts', qc, kc) * jnp.exp(logD - mt[..., None])
        w_c = jnp.exp(inter - mt)
        num = (jnp.einsum('bhts,bshd->bthd', s, vc)
               + jnp.einsum('bthd,bhde->bthe', qc, C) * jnp.swapaxes(w_c, 1, 2)[..., None])
        den = jnp.sum(s, -1) + w_c * jnp.einsum('bthd,bhd->bht', qc, n)
        h = num / jnp.swapaxes(jnp.maximum(jnp.abs(den), jnp.exp(-mt)), 1, 2)[..., None]
        bL = b[..., -1]
        tail = bL[..., None] - b + li
        m_new = jnp.maximum(bL + m, jnp.max(tail, -1))
        wk = jnp.exp(tail - m_new[..., None])
        sc = jnp.exp(bL + m - m_new)
        C = sc[..., None, None] * C + jnp.einsum('bhs,bshd,bshe->bhde', wk, kc, vc)
        n = sc[..., None] * n + jnp.einsum('bhs,bshd->bhd', wk, kc)
        return (C, n, m_new), h

    xs = tuple(to_chunks(a, L) for a in (q, k, v, i_pre, lf))
    (C, n, m), h = lax.scan(step, (C0, n0, m0), xs)
    return from_chunks(h), C, n, m


def s5_combine(e1, e2):
    a1r, a1i, b1r, b1i = e1
    a2r, a2i, b2r, b2i = e2
    return (a1r * a2r - a1i * a2i, a1r * a2i + a1i * a2r,
            a2r * b1r - a2i * b1i + b2r, a2r * b1i + a2i * b1r + b2i)


def s5_mixer(u, p, h0_re, h0_im):
    B, T, _ = u.shape
    f32 = jnp.float32
    lam_re = p['s5_lam_re'].astype(f32)
    lam_im = p['s5_lam_im'].astype(f32)
    dt = jnp.exp(p['s5_log_dt'].astype(f32))[:, None]
    mag = jnp.exp(lam_re * dt)
    lb_re = mag * jnp.cos(lam_im * dt)
    lb_im = mag * jnp.sin(lam_im * dt)
    nr = lb_re - 1.0
    den = lam_re * lam_re + lam_im * lam_im
    f_re = (nr * lam_re + lb_im * lam_im) / den
    f_im = (lb_im * lam_re - nr * lam_im) / den
    B_re = p['s5_B_re'].astype(f32)
    B_im = p['s5_B_im'].astype(f32)
    Bb_re = f_re[..., None] * B_re - f_im[..., None] * B_im
    Bb_im = f_re[..., None] * B_im + f_im[..., None] * B_re
    ug = u.reshape(B, T, S5_GROUPS, S5_CH)
    bu_re = jnp.einsum('btgc,gpc->btgp', ug, Bb_re)
    bu_im = jnp.einsum('btgc,gpc->btgp', ug, Bb_im)
    a_re = jnp.broadcast_to(lb_re, bu_re.shape)
    a_im = jnp.broadcast_to(lb_im, bu_im.shape)
    A_re, A_im, H_re, H_im = lax.associative_scan(s5_combine, (a_re, a_im, bu_re, bu_im), axis=1)
    h_re = H_re + A_re * h0_re[:, None] - A_im * h0_im[:, None]
    h_im = H_im + A_re * h0_im[:, None] + A_im * h0_re[:, None]
    y = (jnp.einsum('gcp,btgp->btgc', p['s5_C_re'].astype(f32), h_re)
         - jnp.einsum('gcp,btgp->btgc', p['s5_C_im'].astype(f32), h_im))
    y = y.reshape(B, T, W_GRP) + p['s5_D'] * u
    y = jax.nn.gelu(y)
    y = y * jax.nn.sigmoid(y @ p['s5_w_glu'] + p['s5_b_glu'])
    return y, h_re[:, -1], h_im[:, -1]


def rwkv7_recurrence(r, k, v, ld, kk, a, S0):
    def step(S, inp):
        rt, kt, vt, ldt, kkt, at = inp
        S = (S * jnp.exp(ldt)[:, :, None, :]
             - jnp.einsum('bhvk,bhk->bhv', S, kkt)[..., None] * (kkt * at)[:, :, None, :]
             + vt[..., None] * kt[:, :, None, :])
        return S, jnp.einsum('bhvk,bhk->bhv', S, rt)
    xs = tuple(jnp.moveaxis(t, 1, 0) for t in (r, k, v, ld, kk, a))
    S, y = lax.scan(step, S0, xs)
    return jnp.moveaxis(y, 0, 1), S


def rwkv7_mixer(pr, p, S0, prev):
    B, T, _ = pr.shape
    shifted = jnp.concatenate([prev[:, None], pr[:, :-1]], axis=1)
    xm = pr + (shifted - pr) * p['rwkv_mu']
    r, k, v, wl, al, gl = jnp.split(
        xm, [W_GRP, 2 * W_GRP, 3 * W_GRP, 3 * W_GRP + R_LR_W, 3 * W_GRP + R_LR_W + R_LR_A], axis=-1)
    w = -jax.nn.softplus(-(p['rwkv_w0'] + jnp.tanh(wl) @ p['rwkv_w2'])) - 0.5
    log_decay = -jnp.exp(w)
    a = jax.nn.sigmoid(p['rwkv_a0'] + al @ p['rwkv_a2'])
    g = jax.nn.sigmoid(gl) @ p['rwkv_g2']
    hd = lambda t: t.reshape(B, T, R_HEADS, R_DH)
    kk = l2norm(hd(k * p['rwkv_k_k']))
    k = k * (1.0 + (a - 1.0) * p['rwkv_k_a'])
    rh, kh, vh = hd(r), hd(k), hd(v)
    y, S = rwkv7_recurrence(rh, kh, vh, hd(log_decay), kk, hd(a), S0)
    mu = jnp.mean(y, -1, keepdims=True)
    var = jnp.var(y, -1, keepdims=True)
    y = ((y - mu) * lax.rsqrt(var + GN_EPS) * p['rwkv_ln_w'].reshape(R_HEADS, R_DH)
         + p['rwkv_ln_b'].reshape(R_HEADS, R_DH))
    y = y + jnp.sum(rh * kh * p['rwkv_r_k'], -1, keepdims=True) * vh
    return y.reshape(B, T, W_GRP) * g, S, pr[:, -1]


def gdn_chunkwise(q, k, v, g, beta, S0):
    L = min(CHUNK, q.shape[1])
    strict = jnp.tril(jnp.ones((L, L), dtype=bool), -1)
    incl = jnp.tril(jnp.ones((L, L), dtype=bool))
    eye = jnp.eye(L, dtype=jnp.float32)
    Dv = v.shape[-1]

    def step(S, inp):
        qc, kc, vc, gc, bc = inp
        G = jnp.cumsum(jnp.swapaxes(gc, 1, 2), axis=-1)
        bh = jnp.swapaxes(bc, 1, 2)
        diff = G[..., :, None] - G[..., None, :]
        A = bh[..., None] * jnp.einsum('bthd,bshd->bhts', kc, kc) * jnp.exp(jnp.where(strict, diff, -jnp.inf))
        rhs = jnp.concatenate([bh[..., None] * jnp.swapaxes(vc, 1, 2),
                               (bh * jnp.exp(G))[..., None] * jnp.swapaxes(kc, 1, 2)], axis=-1)
        sol = lax.linalg.triangular_solve(eye + A, rhs, left_side=True, lower=True, unit_diagonal=True)
        U = sol[..., :Dv] - jnp.einsum('bhtd,bhde->bhte', sol[..., Dv:], S)
        qk = jnp.einsum('bthd,bshd->bhts', qc, kc) * jnp.exp(jnp.where(incl, diff, -jnp.inf))
        o = (jnp.exp(G)[..., None] * jnp.einsum('bthd,bhde->bhte', qc, S)
             + jnp.einsum('bhts,bhse->bhte', qk, U))
        GL = G[..., -1]
        S = (jnp.exp(GL)[..., None, None] * S
             + jnp.einsum('bhs,bshd,bhse->bhde', jnp.exp(GL[..., None] - G), kc, U))
        return S, jnp.swapaxes(o, 1, 2)

    xs = tuple(to_chunks(a, L) for a in (q, k, v, g, beta))
    S, o = lax.scan(step, S0, xs)
    return from_chunks(o), S


def gdn_mixer(pg, p, S0, conv_buf):
    B, T, _ = pg.shape
    qkv, z, a_in, b_in = jnp.split(pg, [3 * W_GRP, 4 * W_GRP, 4 * W_GRP + G_HEADS], axis=-1)
    xp = jnp.concatenate([conv_buf, qkv], axis=1)
    cw = p['gdn_conv_w']
    acc = xp[:, 0:T] * cw[0]
    for j in range(1, CONV_W):
        acc = acc + xp[:, j:j + T] * cw[j]
    acc = jax.nn.silu(acc)
    q, k, v = jnp.split(acc, [W_GRP, 2 * W_GRP], axis=-1)
    hd = lambda t: t.reshape(B, T, G_HEADS, G_DH)
    q = l2norm(hd(q)) * (G_DH ** -0.5)
    k = l2norm(hd(k))
    g = -jnp.exp(p['gdn_A_log']) * jax.nn.softplus(a_in + p['gdn_dt_bias'])
    beta = jax.nn.sigmoid(b_in)
    o, S = gdn_chunkwise(q, k, hd(v), g, beta, S0)
    o = o * lax.rsqrt(jnp.mean(o * o, -1, keepdims=True) + EPS) * p['gdn_norm_g']
    o = o * jax.nn.silu(hd(z))
    return o.reshape(B, T, W_GRP), S, xp[:, -(CONV_W - 1):]


def zero_state(b):
    z = lambda *s: jnp.zeros(s, jnp.float32)
    return (z(b, M_HEADS, M_DH, M_DH), z(b, M_HEADS, M_DH), z(b, M_HEADS),
            z(b, S5_GROUPS, S5_P), z(b, S5_GROUPS, S5_P),
            z(b, R_HEADS, R_DH, R_DH), z(b, N_R),
            z(b, G_HEADS, G_DH, G_DH), z(b, CONV_W - 1, 3 * W_GRP))


def trunk_layer(x, p, state):
    f32 = jnp.float32
    mC0, mn0, mm0, s5r0, s5i0, rS0, rsh0, gS0, gcv0 = [s.astype(f32) for s in state]
    B, T, _ = x.shape
    h = rmsnorm(x, p['g_pre_mix'])
    proj = (h @ p['w_in']).astype(f32)
    pm, ps, pr, pg = jnp.split(proj, [N_M, N_M + N_S, N_M + N_S + N_R], axis=-1)
    mq, mk, mv, mo, mi, mf = jnp.split(pm, [W_GRP, 2 * W_GRP, 3 * W_GRP, 4 * W_GRP, 4 * W_GRP + M_HEADS], axis=-1)
    mh = lambda t: t.reshape(B, T, M_HEADS, M_DH)
    gb = p['mlstm_gate_bias'].astype(f32)
    hm, mC, mn, mm = mlstm_chunkwise(mh(mq), mh(mk), mh(mv), mi + gb[0], mf + gb[1], mC0, mn0, mm0)
    hm = (hm - jnp.mean(hm, -1, keepdims=True)) * lax.rsqrt(jnp.var(hm, -1, keepdims=True) + EPS)
    y_m = jax.nn.sigmoid(mo) * (hm.reshape(B, T, W_GRP) * p['mlstm_norm_g'])
    y_s, s5r, s5i = s5_mixer(ps, p, s5r0, s5i0)
    y_r, rS, rsh = rwkv7_mixer(pr, p, rS0, rsh0)
    y_g, gS, gcv = gdn_mixer(pg, p, gS0, gcv0)
    mix = jnp.concatenate([y_m, y_s, y_r, y_g], axis=-1).astype(x.dtype)
    x = x + rmsnorm(mix @ p['w_out'], p['g_post_mix'])
    hf = rmsnorm(x, p['g_pre_ffn'])
    ff = (jax.nn.silu(hf @ p['w_gate']) * (hf @ p['w_up'])) @ p['w_down']
    x = x + rmsnorm(ff, p['g_post_ffn'])
    return x, (mC, mn, mm, s5r, s5i, rS, rsh, gS, gcv)


def setup_inputs(seed: int = 0) -> dict:
    key = jax.random.key(seed)
    ks = iter(jax.random.split(key, 64))
    f32 = jnp.float32
    L = DEPTH

    def nrm(shape, scale=1.0):
        return jax.random.normal(next(ks), shape, f32) * scale

    def unif(shape, lo, hi):
        return jax.random.uniform(next(ks), shape, f32, lo, hi)

    x_prompt = nrm((BATCH, SEQ, D_MODEL))
    x_sample = nrm((DEC_BATCH, DEC_SEQ, D_MODEL))
    state_mlstm_C = nrm((L, DEC_BATCH, M_HEADS, M_DH, M_DH), 0.1)
    state_mlstm_n = nrm((L, DEC_BATCH, M_HEADS, M_DH), 0.1)
    state_mlstm_m = nrm((L, DEC_BATCH, M_HEADS), 0.5)
    state_s5_re = nrm((L, DEC_BATCH, S5_GROUPS, S5_P))
    state_s5_im = nrm((L, DEC_BATCH, S5_GROUPS, S5_P))
    state_rwkv_S = nrm((L, DEC_BATCH, R_HEADS, R_DH, R_DH), 0.5)
    state_rwkv_shift = nrm((L, DEC_BATCH, N_R))
    state_gdn_S = nrm((L, DEC_BATCH, G_HEADS, G_DH, G_DH), 0.1)
    state_gdn_conv = nrm((L, DEC_BATCH, CONV_W - 1, 3 * W_GRP))
    g_pre_mix = 1.0 + nrm((L, D_MODEL), 0.01)
    w_in = nrm((L, D_MODEL, N_IN), D_MODEL ** -0.5)
    mlstm_gate_bias = jnp.stack([-2.0 + nrm((L, M_HEADS), 0.1),
                                 jnp.linspace(3.0, 6.0, M_HEADS) + nrm((L, M_HEADS), 0.1)], axis=1)
    mlstm_norm_g = 1.0 + nrm((L, W_GRP), 0.01)
    s5_lam_re = -0.5 + nrm((L, S5_GROUPS, S5_P), 0.01)
    s5_lam_im = jnp.pi * jnp.arange(S5_P, dtype=f32) + nrm((L, S5_GROUPS, S5_P), 0.01)
    s5_log_dt = unif((L, S5_GROUPS), math.log(1e-3), math.log(1e-1))
    s5_B_re = nrm((L, S5_GROUPS, S5_P, S5_CH), (2 * S5_CH) ** -0.5)
    s5_B_im = nrm((L, S5_GROUPS, S5_P, S5_CH), (2 * S5_CH) ** -0.5)
    s5_C_re = nrm((L, S5_GROUPS, S5_CH, S5_P), (2 * S5_P) ** -0.5)
    s5_C_im = nrm((L, S5_GROUPS, S5_CH, S5_P), (2 * S5_P) ** -0.5)
    s5_D = nrm((L, W_GRP))
    s5_w_glu = nrm((L, W_GRP, W_GRP), W_GRP ** -0.5)
    s5_b_glu = nrm((L, W_GRP), 0.01)
    rwkv_mu = unif((L, N_R), 0.0, 1.0)
    rwkv_w0 = jnp.tile(jnp.linspace(-6.0, 1.0, R_DH), R_HEADS) + nrm((L, W_GRP), 0.1)
    rwkv_w2 = nrm((L, R_LR_W, W_GRP), 0.1 * R_LR_W ** -0.5)
    rwkv_a0 = nrm((L, W_GRP), 0.1)
    rwkv_a2 = nrm((L, R_LR_A, W_GRP), 0.1 * R_LR_A ** -0.5)
    rwkv_g2 = nrm((L, R_LR_G, W_GRP), R_LR_G ** -0.5)
    rwkv_k_k = 0.85 + nrm((L, W_GRP), 0.01)
    rwkv_k_a = 1.0 + nrm((L, W_GRP), 0.01)
    rwkv_r_k = nrm((L, R_HEADS, R_DH), 0.1)
    rwkv_ln_w = 1.0 + nrm((L, W_GRP), 0.01)
    rwkv_ln_b = nrm((L, W_GRP), 0.01)
    gdn_conv_w = nrm((L, CONV_W, 3 * W_GRP), CONV_W ** -0.5)
    gdn_A_log = jnp.log(unif((L, G_HEADS), 1.0, 16.0))
    dt = jnp.exp(unif((L, G_HEADS), math.log(1e-3), math.log(1e-1)))
    gdn_dt_bias = dt + jnp.log(-jnp.expm1(-dt))
    gdn_norm_g = 1.0 + nrm((L, G_DH), 0.01)
    w_out = nrm((L, W_MIX, D_MODEL), W_MIX ** -0.5)
    g_post_mix = 1.0 + nrm((L, D_MODEL), 0.01)
    g_pre_ffn = 1.0 + nrm((L, D_MODEL), 0.01)
    w_gate = nrm((L, D_MODEL, D_FF), D_MODEL ** -0.5)
    w_up = nrm((L, D_MODEL, D_FF), D_MODEL ** -0.5)
    w_down = nrm((L, D_FF, D_MODEL), D_FF ** -0.5)
    g_post_ffn = 1.0 + nrm((L, D_MODEL), 0.01)
    return {
        'x_prompt': x_prompt, 'x_sample': x_sample,
        'state_mlstm_C': state_mlstm_C, 'state_mlstm_n': state_mlstm_n, 'state_mlstm_m': state_mlstm_m,
        'state_s5_re': state_s5_re, 'state_s5_im': state_s5_im,
        'state_rwkv_S': state_rwkv_S, 'state_rwkv_shift': state_rwkv_shift,
        'state_gdn_S': state_gdn_S, 'state_gdn_conv': state_gdn_conv,
        'g_pre_mix': g_pre_mix, 'w_in': w_in, 'mlstm_gate_bias': mlstm_gate_bias, 'mlstm_norm_g': mlstm_norm_g,
        's5_lam_re': s5_lam_re, 's5_lam_im': s5_lam_im, 's5_log_dt': s5_log_dt,
        's5_B_re': s5_B_re, 's5_B_im': s5_B_im, 's5_C_re': s5_C_re, 's5_C_im': s5_C_im,
        's5_D': s5_D, 's5_w_glu': s5_w_glu, 's5_b_glu': s5_b_glu,
        'rwkv_mu': rwkv_mu, 'rwkv_w0': rwkv_w0, 'rwkv_w2': rwkv_w2, 'rwkv_a0': rwkv_a0, 'rwkv_a2': rwkv_a2,
        'rwkv_g2': rwkv_g2, 'rwkv_k_k': rwkv_k_k, 'rwkv_k_a': rwkv_k_a, 'rwkv_r_k': rwkv_r_k,
        'rwkv_ln_w': rwkv_ln_w, 'rwkv_ln_b': rwkv_ln_b,
        'gdn_conv_w': gdn_conv_w, 'gdn_A_log': gdn_A_log, 'gdn_dt_bias': gdn_dt_bias, 'gdn_norm_g': gdn_norm_g,
        'w_out': w_out, 'g_post_mix': g_post_mix, 'g_pre_ffn': g_pre_ffn,
        'w_gate': w_gate, 'w_up': w_up, 'w_down': w_down, 'g_post_ffn': g_post_ffn,
    }


def reference(x_prompt, x_sample, state_mlstm_C, state_mlstm_n, state_mlstm_m, state_s5_re, state_s5_im,
              state_rwkv_S, state_rwkv_shift, state_gdn_S, state_gdn_conv,
              g_pre_mix, w_in, mlstm_gate_bias, mlstm_norm_g,
              s5_lam_re, s5_lam_im, s5_log_dt, s5_B_re, s5_B_im, s5_C_re, s5_C_im, s5_D, s5_w_glu, s5_b_glu,
              rwkv_mu, rwkv_w0, rwkv_w2, rwkv_a0, rwkv_a2, rwkv_g2, rwkv_k_k, rwkv_k_a, rwkv_r_k,
              rwkv_ln_w, rwkv_ln_b, gdn_conv_w, gdn_A_log, gdn_dt_bias, gdn_norm_g,
              w_out, g_post_mix, g_pre_ffn, w_gate, w_up, w_down, g_post_ffn):
    caches = (state_mlstm_C, state_mlstm_n, state_mlstm_m, state_s5_re, state_s5_im,
              state_rwkv_S, state_rwkv_shift, state_gdn_S, state_gdn_conv)
    yp = x_prompt
    ys = x_sample
    outs_p = []
    outs_s = []
    for l in range(DEPTH):
        p = {
            'g_pre_mix': g_pre_mix[l], 'w_in': w_in[l],
            'mlstm_gate_bias': mlstm_gate_bias[l], 'mlstm_norm_g': mlstm_norm_g[l],
            's5_lam_re': s5_lam_re[l], 's5_lam_im': s5_lam_im[l], 's5_log_dt': s5_log_dt[l],
            's5_B_re': s5_B_re[l], 's5_B_im': s5_B_im[l], 's5_C_re': s5_C_re[l], 's5_C_im': s5_C_im[l],
            's5_D': s5_D[l], 's5_w_glu': s5_w_glu[l], 's5_b_glu': s5_b_glu[l],
            'rwkv_mu': rwkv_mu[l], 'rwkv_w0': rwkv_w0[l], 'rwkv_w2': rwkv_w2[l], 'rwkv_a0': rwkv_a0[l],
            'rwkv_a2': rwkv_a2[l], 'rwkv_g2': rwkv_g2[l], 'rwkv_k_k': rwkv_k_k[l], 'rwkv_k_a': rwkv_k_a[l],
            'rwkv_r_k': rwkv_r_k[l], 'rwkv_ln_w': rwkv_ln_w[l], 'rwkv_ln_b': rwkv_ln_b[l],
            'gdn_conv_w': gdn_conv_w[l], 'gdn_A_log': gdn_A_log[l], 'gdn_dt_bias': gdn_dt_bias[l],
            'gdn_norm_g': gdn_norm_g[l],
            'w_out': w_out[l], 'g_post_mix': g_post_mix[l], 'g_pre_ffn': g_pre_ffn[l],
            'w_gate': w_gate[l], 'w_up': w_up[l], 'w_down': w_down[l], 'g_post_ffn': g_post_ffn[l],
        }
        yp, st_p = trunk_layer(yp, p, zero_state(x_prompt.shape[0]))
        ys, st_s = trunk_layer(ys, p, tuple(c[l] for c in caches))
        outs_p.append(st_p)
        outs_s.append(st_s)
    mC_p, mn_p, mm_p, s5r_p, s5i_p, rS_p, rsh_p, gS_p, gcv_p = [jnp.stack(t) for t in zip(*outs_p)]
    mC_s, mn_s, mm_s, s5r_s, s5i_s, rS_s, rsh_s, gS_s, gcv_s = [jnp.stack(t) for t in zip(*outs_s)]
    return (yp, ys,
            mC_p, mn_p, mm_p, s5r_p, s5i_p, rS_p, rsh_p, gS_p, gcv_p,
            mC_s, mn_s, mm_s, s5r_s, s5i_s, rS_s, rsh_s, gS_s, gcv_s)
```

```python
import functools
import math

import jax
import jax.numpy as jnp
from jax import lax
from jax.experimental import pallas as pl
from jax.experimental.pallas import tpu as pltpu

F32 = jnp.float32
BF16 = jnp.bfloat16

D_MODEL = 2048
DEPTH = 2
EPS = 1e-6
GN_EPS = 64e-5
W_GRP = D_MODEL // 4
M_HEADS, M_DH = 4, 128
S5_CH, S5_GROUPS, S5_P = 16, 32, 64
S5_N = S5_GROUPS * S5_P
R_DH, R_HEADS = 64, 8
R_LR_W, R_LR_A, R_LR_G = 64, 64, 128
G_HEADS, G_DH = 4, 128
CONV_W = 4
D_FF = 5632
N_M = 4 * W_GRP + 2 * M_HEADS
N_S = W_GRP
N_R = 3 * W_GRP + R_LR_W + R_LR_A + R_LR_G
N_G = 4 * W_GRP + 2 * G_HEADS

COL_M = 0
COL_G = 2048
COL_R = 4096
COL_GATE = COL_R + N_R
COL_S = 6144
N_PROJ = 6656
GATE_MI, GATE_MF, GATE_GA, GATE_GB = 0, 4, 8, 12

CHUNK = 64
LANES = 128
VMEM_LIMIT = 56 * 1024 * 1024


def _bf(x):
    return x.astype(BF16)


def _dot(a, b):
    return jnp.dot(_bf(a), _bf(b), preferred_element_type=F32)


def _dot_nt(a, b):
    return lax.dot_general(_bf(a), _bf(b), (((1,), (1,)), ((), ())), preferred_element_type=F32)


def _dot_tn(a, b):
    return lax.dot_general(_bf(a), _bf(b), (((0,), (0,)), ((), ())), preferred_element_type=F32)


def _split3(x):
    hi = _bf(x)
    r1 = x - hi.astype(F32)
    mid = _bf(r1)
    lo = _bf(r1 - mid.astype(F32))
    return hi, mid, lo


def _dot_exact_rhs(a_bf, x):
    hi, mid, lo = _split3(x)
    d = lambda v: jnp.dot(a_bf, v, preferred_element_type=F32)
    return d(hi) + d(mid) + d(lo)


def _dot_exact_lhs(x, b_bf):
    hi, mid, lo = _split3(x)
    d = lambda v: jnp.dot(v, b_bf, preferred_element_type=F32)
    return d(hi) + d(mid) + d(lo)


def _dot_nt_exact_rhs(a_bf, x):
    hi, mid, lo = _split3(x)
    d = lambda v: lax.dot_general(a_bf, v, (((1,), (1,)), ((), ())), preferred_element_type=F32)
    return d(hi) + d(mid) + d(lo)


def _dot_hi(a, b):
    a_hi = _bf(a)
    a_lo = _bf(a - a_hi.astype(F32))
    b_hi = _bf(b)
    b_lo = _bf(b - b_hi.astype(F32))
    d = lambda u, v: jnp.dot(u, v, preferred_element_type=F32)
    return d(a_hi, b_hi) + d(a_hi, b_lo) + d(a_lo, b_hi)


def _sigmoid(x):
    return 1.0 / (1.0 + jnp.exp(-x))


def _softplus(x):
    return jnp.maximum(x, 0.0) + jnp.log(1.0 + jnp.exp(-jnp.abs(x)))


def _log_sigmoid(x):
    return -_softplus(-x)


def _silu(x):
    return x * _sigmoid(x)


def _gelu_tanh(x):
    return 0.5 * x * (1.0 + jnp.tanh(math.sqrt(2.0 / math.pi) * (x + 0.044715 * (x * x * x))))


def _rms(x, g):
    return x * lax.rsqrt(jnp.mean(x * x, -1, keepdims=True) + EPS) * g


def _iota2(shape, axis):
    return lax.broadcasted_iota(jnp.int32, shape, axis)


def _tri_masks(n):
    r, c = _iota2((n, n), 0), _iota2((n, n), 1)
    return r >= c, r > c


def _ones_where(mask):
    return jnp.where(mask, 1.0, 0.0).astype(BF16)


def _unit_lower_inverse(n_mat, size):
    eye = jnp.where(_iota2((size, size), 0) == _iota2((size, size), 1), 1.0, 0.0).astype(F32)
    t = eye + n_mat
    p = n_mat
    k = 1
    while 2 * k < size:
        p = _dot_hi(p, p)
        t = t + _dot_hi(t, p)
        k *= 2
    return t


def _col(x, j):
    return x[:, j:j + 1]


def _row(x, i):
    return x[i:i + 1, :]


def _params(sem):
    return pltpu.CompilerParams(dimension_semantics=sem, vmem_limit_bytes=VMEM_LIMIT)


def _proj_body(x_ref, g_ref, w_ref, o_ref, h_ref):
    @pl.when(pl.program_id(1) == 0)
    def _():
        h_ref[...] = _bf(_rms(x_ref[...], g_ref[...]))
    o_ref[...] = jnp.dot(h_ref[...], w_ref[...], preferred_element_type=F32)


def _proj(x2, g, w):
    n = x2.shape[0]
    tm = min(512, n)
    tn = N_PROJ // 4
    return pl.pallas_call(
        _proj_body,
        grid=(n // tm, N_PROJ // tn),
        in_specs=[pl.BlockSpec((tm, D_MODEL), lambda i, j: (i, 0)),
                  pl.BlockSpec((1, D_MODEL), lambda i, j: (0, 0)),
                  pl.BlockSpec((D_MODEL, tn), lambda i, j: (0, j))],
        out_specs=pl.BlockSpec((tm, tn), lambda i, j: (i, j)),
        out_shape=jax.ShapeDtypeStruct((n, N_PROJ), F32),
        scratch_shapes=[pltpu.VMEM((tm, D_MODEL), BF16)],
        compiler_params=_params(("arbitrary", "arbitrary")),
        name="in_proj",
    )(x2, g, w)


def _mlstm_body(p_ref, gt_ref, gb_ref, ng_ref, c0_ref, n0_ref, m0_ref,
                y_ref, c_ref, n_ref, m_ref, *, bb, L):
    @pl.when(pl.program_id(1) == 0)
    def _():
        c_ref[...] = c0_ref[...]
        n_ref[...] = n0_ref[...]
        m_ref[...] = m0_ref[...]

    incl, _ = _tri_masks(L)
    tri_lo = _ones_where(incl)
    tri_up = _ones_where(_iota2((L, L), 0) <= _iota2((L, L), 1))
    sel = _ones_where(_iota2((8, LANES), 0) == _iota2((8, LANES), 1))
    lane = _iota2((1, LANES), 1)
    scale = M_DH ** -0.5

    for b in range(bb):
        gates = gt_ref[b] + gb_ref[...]
        lf_cols = _log_sigmoid(gates)
        b_cols = _dot_exact_rhs(tri_lo, lf_cols)
        rows = _dot_nt_exact_rhs(sel, jnp.where(lane < GATE_MF, gates, lf_cols))
        b_rows = _dot_exact_lhs(rows, tri_up)
        m_row = m_ref[b]
        m_out = m_row
        for h in range(M_HEADS):
            sl = slice(h * M_DH, (h + 1) * M_DH)
            q = p_ref[b, :, sl] * scale
            k = p_ref[b, :, W_GRP + h * M_DH:W_GRP + (h + 1) * M_DH]
            v = p_ref[b, :, 2 * W_GRP + h * M_DH:2 * W_GRP + (h + 1) * M_DH]
            o = p_ref[b, :, 3 * W_GRP + h * M_DH:3 * W_GRP + (h + 1) * M_DH]
            li_c = _col(gates, GATE_MI + h)
            b_c = _col(b_cols, GATE_MF + h)
            li_r = _row(rows, GATE_MI + h)
            b_r = _row(b_rows, GATE_MF + h)
            m_prev = jnp.sum(jnp.where(lane == h, m_row, 0.0), axis=1, keepdims=True)
            c_st = c_ref[b, h]
            n_st = n_ref[b, h:h + 1, :]

            log_d = jnp.where(incl, b_c - b_r + li_r, -jnp.inf)
            inter = b_c + m_prev
            mt = jnp.maximum(inter, jnp.max(log_d, axis=-1, keepdims=True))
            s = _dot_nt(q, k) * jnp.exp(log_d - mt)
            w_c = jnp.exp(inter - mt)
            num = _dot(s, v) + _dot(q, c_st) * w_c
            den = jnp.sum(s, -1, keepdims=True) + w_c * jnp.sum(q * n_st, -1, keepdims=True)
            hh = num / jnp.maximum(jnp.abs(den), jnp.exp(-mt))

            b_last = b_c[L - 1:L, :]
            tail = b_last - b_c + li_c
            m_new = jnp.maximum(b_last + m_prev, jnp.max(tail, axis=0, keepdims=True))
            wk = jnp.exp(tail - m_new)
            sc = jnp.exp(b_last + m_prev - m_new)
            c_ref[b, h] = sc * c_st + _dot_tn(k, wk * v)
            n_ref[b, h:h + 1, :] = sc * n_st + jnp.sum(wk * k, axis=0, keepdims=True)
            m_out = jnp.where(lane == h, m_new, m_out)

            mu = jnp.mean(hh, -1, keepdims=True)
            hc = hh - mu
            hn = hc * lax.rsqrt(jnp.mean(hc * hc, -1, keepdims=True) + EPS)
            y_ref[b, :, sl] = _bf(_sigmoid(o) * (hn * ng_ref[:, sl]))
        m_ref[b] = m_out


def _mlstm(proj3, gate_bias, norm_g, c0, n0, m0):
    B, T, _ = proj3.shape
    L = min(CHUNK, T)
    bb = 2
    st = lambda b, t: (b, 0, 0)
    return pl.pallas_call(
        functools.partial(_mlstm_body, bb=bb, L=L),
        grid=(B // bb, T // L),
        in_specs=[pl.BlockSpec((bb, L, 4 * W_GRP), lambda b, t: (b, t, COL_M // (4 * W_GRP))),
                  pl.BlockSpec((bb, L, LANES), lambda b, t: (b, t, COL_GATE // LANES)),
                  pl.BlockSpec((1, LANES), lambda b, t: (0, 0)),
                  pl.BlockSpec((1, W_GRP), lambda b, t: (0, 0)),
                  pl.BlockSpec((bb, M_HEADS, M_DH, M_DH), lambda b, t: (b, 0, 0, 0)),
                  pl.BlockSpec((bb, M_HEADS, M_DH), st),
                  pl.BlockSpec((bb, 1, LANES), st)],
        out_specs=[pl.BlockSpec((bb, L, W_GRP), lambda b, t: (b, t, 0)),
                   pl.BlockSpec((bb, M_HEADS, M_DH, M_DH), lambda b, t: (b, 0, 0, 0)),
                   pl.BlockSpec((bb, M_HEADS, M_DH), st),
                   pl.BlockSpec((bb, 1, LANES), st)],
        out_shape=[jax.ShapeDtypeStruct((B, T, W_GRP), BF16),
                   jax.ShapeDtypeStruct((B, M_HEADS, M_DH, M_DH), F32),
                   jax.ShapeDtypeStruct((B, M_HEADS, M_DH), F32),
                   jax.ShapeDtypeStruct((B, 1, LANES), F32)],
        compiler_params=_params(("arbitrary", "arbitrary")),
        name="mlstm",
    )(proj3, proj3, gate_bias, norm_g, c0, n0, m0)


def _gdn_body(p_ref, gt_ref, alog_ref, dtb_ref, cw_ref, ng_ref, s0_ref, cv0_ref,
              y_ref, s_ref, cv_ref, xp_ref, *, bb, L):
    t = pl.program_id(1)
    nt = pl.num_programs(1)
    W3 = 3 * W_GRP
    PAD = 8

    @pl.when(t == 0)
    def _():
        s_ref[...] = s0_ref[...]
        xp_ref[:, PAD - (CONV_W - 1):PAD, :] = cv0_ref[...]

    incl, strict = _tri_masks(L)
    tri_lo = _ones_where(incl)
    tri_up = _ones_where(_iota2((L, L), 0) <= _iota2((L, L), 1))
    sel = _ones_where(_iota2((16, LANES), 0) == _iota2((16, LANES), 1))
    neg_a = -jnp.exp(alog_ref[...])

    for b in range(bb):
        xp_ref[b, PAD:PAD + L, :] = p_ref[b, :, 0:W3]
        acc = xp_ref[b, PAD - 3:PAD - 3 + L, :] * cw_ref[0:1, :]
        for j in range(1, CONV_W):
            acc = acc + xp_ref[b, PAD - 3 + j:PAD - 3 + j + L, :] * cw_ref[j:j + 1, :]
        tail3 = xp_ref[b, PAD + L - (CONV_W - 1):PAD + L, :]
        xp_ref[b, PAD - (CONV_W - 1):PAD, :] = tail3

        @pl.when(t == nt - 1)
        def _():
            cv_ref[b] = tail3

        acc = _silu(acc)
        gates = gt_ref[b]
        g_cols = neg_a * _softplus(gates + dtb_ref[...])
        beta_cols = _sigmoid(gates)
        gc_cols = _dot_exact_rhs(tri_lo, g_cols)
        g_rows = _dot_nt_exact_rhs(sel, g_cols)
        gc_rows = _dot_exact_lhs(g_rows, tri_up)

        for h in range(G_HEADS):
            sl = slice(h * G_DH, (h + 1) * G_DH)
            q = acc[:, h * G_DH:(h + 1) * G_DH]
            k = acc[:, W_GRP + h * G_DH:W_GRP + (h + 1) * G_DH]
            v = acc[:, 2 * W_GRP + h * G_DH:2 * W_GRP + (h + 1) * G_DH]
            z = p_ref[b, :, W3 + h * G_DH:W3 + (h + 1) * G_DH]
            q = q * lax.rsqrt(jnp.sum(q * q, -1, keepdims=True) + 1e-6) * (G_DH ** -0.5)
            k = k * lax.rsqrt(jnp.sum(k * k, -1, keepdims=True) + 1e-6)
            g_c = _col(gc_cols, GATE_GA + h)
            g_r = _row(gc_rows, GATE_GA + h)
            beta = _col(beta_cols, GATE_GB + h)
            s_st = s_ref[b, h]

            diff = g_c - g_r
            eg = jnp.exp(g_c)
            a_mat = beta * _dot_nt(k, k) * jnp.exp(jnp.where(strict, diff, -jnp.inf))
            t_inv = _unit_lower_inverse(-a_mat, L)
            sol_v = _dot_hi(t_inv, beta * v)
            sol_k = _dot_hi(t_inv, (beta * eg) * k)
            u = sol_v - _dot(sol_k, s_st)
            qk = _dot_nt(q, k) * jnp.exp(jnp.where(incl, diff, -jnp.inf))
            o = eg * _dot(q, s_st) + _dot(qk, u)
            g_last = g_c[L - 1:L, :]
            s_ref[b, h] = jnp.exp(g_last) * s_st + _dot_tn(k * jnp.exp(g_last - g_c), u)

            o = o * lax.rsqrt(jnp.mean(o * o, -1, keepdims=True) + EPS) * ng_ref[...]
            y_ref[b, :, sl] = _bf(o * _silu(z))


def _gdn(proj3, alog_row, dtb_row, conv_w, norm_g, s0, cv0):
    B, T, _ = proj3.shape
    L = min(CHUNK, T)
    bb = 2
    return pl.pallas_call(
        functools.partial(_gdn_body, bb=bb, L=L),
        grid=(B // bb, T // L),
        in_specs=[pl.BlockSpec((bb, L, 4 * W_GRP), lambda b, t: (b, t, COL_G // (4 * W_GRP))),
                  pl.BlockSpec((bb, L, LANES), lambda b, t: (b, t, COL_GATE // LANES)),
                  pl.BlockSpec((1, LANES), lambda b, t: (0, 0)),
                  pl.BlockSpec((1, LANES), lambda b, t: (0, 0)),
                  pl.BlockSpec((CONV_W, 3 * W_GRP), lambda b, t: (0, 0)),
                  pl.BlockSpec((1, G_DH), lambda b, t: (0, 0)),
                  pl.BlockSpec((bb, G_HEADS, G_DH, G_DH), lambda b, t: (b, 0, 0, 0)),
                  pl.BlockSpec((bb, CONV_W - 1, 3 * W_GRP), lambda b, t: (b, 0, 0))],
        out_specs=[pl.BlockSpec((bb, L, W_GRP), lambda b, t: (b, t, 0)),
                   pl.BlockSpec((bb, G_HEADS, G_DH, G_DH), lambda b, t: (b, 0, 0, 0)),
                   pl.BlockSpec((bb, CONV_W - 1, 3 * W_GRP), lambda b, t: (b, 0, 0))],
        out_shape=[jax.ShapeDtypeStruct((B, T, W_GRP), BF16),
                   jax.ShapeDtypeStruct((B, G_HEADS, G_DH, G_DH), F32),
                   jax.ShapeDtypeStruct((B, CONV_W - 1, 3 * W_GRP), F32)],
        scratch_shapes=[pltpu.VMEM((bb, 8 + L, 3 * W_GRP), F32)],
        compiler_params=_params(("arbitrary", "arbitrary")),
        name="gdn",
    )(proj3, proj3, alog_row, dtb_row, conv_w, norm_g, s0, cv0)


def _rwkv_body(p_ref, mu_ref, wwa_ref, w0_ref, a0_ref, g2_ref, kk_ref, ka_ref, rk_ref,
               lnw_ref, lnb_ref, s0_ref, sh0_ref,
               y_ref, s_ref, sh_ref, xs_ref, *, bb, L):
    t = pl.program_id(1)
    nt = pl.num_programs(1)
    PAD = 8

    @pl.when(t == 0)
    def _():
        s_ref[...] = s0_ref[...]
        xs_ref[:, PAD - 1:PAD, :] = sh0_ref[...]

    incl, strict = _tri_masks(L)
    tri_lo = _ones_where(incl)
    lane = _iota2((1, LANES), 1)
    head_masks = (lane < R_DH, lane >= R_DH)
    r_i, c_i = _iota2((LANES, LANES), 0), _iota2((LANES, LANES), 1)
    same_head = (r_i < R_DH) == (c_i < R_DH)
    seg = _ones_where(same_head)
    seg_sum = lambda x: _dot_exact_lhs(x, seg)

    for b in range(bb):
        xs_ref[b, PAD:PAD + L, :] = p_ref[b, :, 0:N_R]
        cur = p_ref[b, :, 0:N_R]
        prev = xs_ref[b, PAD - 1:PAD - 1 + L, :]
        last = xs_ref[b, PAD + L - 1:PAD + L, :]
        xs_ref[b, PAD - 1:PAD, :] = last

        @pl.when(t == nt - 1)
        def _():
            sh_ref[b] = last

        xm = cur + (prev - cur) * mu_ref[...]
        r_all = xm[:, 0:W_GRP]
        k_all = xm[:, W_GRP:2 * W_GRP]
        v_all = xm[:, 2 * W_GRP:3 * W_GRP]
        wa_code = xm[:, 3 * W_GRP:3 * W_GRP + LANES]
        g_code = xm[:, 3 * W_GRP + LANES:3 * W_GRP + 2 * LANES]
        wa_in = jnp.where(lane < R_LR_W, jnp.tanh(wa_code), wa_code)
        wa = _dot(wa_in, wwa_ref[...])
        w_all = -_softplus(-(w0_ref[...] + wa[:, 0:W_GRP])) - 0.5
        ld_all = -jnp.exp(w_all)
        a_all = _sigmoid(a0_ref[...] + wa[:, W_GRP:2 * W_GRP])
        g_all = _dot(_sigmoid(g_code), g2_ref[...])
        k2_all = k_all * (1.0 + (a_all - 1.0) * ka_ref[...])
        kkr_all = k_all * kk_ref[...]

        for p in range(R_HEADS // 2):
            sl = slice(p * LANES, (p + 1) * LANES)
            r, k, v, ld, a, kkr = r_all[:, sl], k2_all[:, sl], v_all[:, sl], ld_all[:, sl], a_all[:, sl], kkr_all[:, sl]
            kk = kkr * lax.rsqrt(seg_sum(kkr * kkr) + 1e-6)
            cum = _dot_exact_rhs(tri_lo, ld)
            e_neg = jnp.exp(-cum)
            a_t = jnp.exp(cum - ld) * kk
            b_t = -(kk * a) * e_neg
            k_t = k * e_neg
            r_t = r * jnp.exp(cum)
            s_st = s_ref[b, p]

            u = jnp.zeros((L, LANES), F32)
            intra = []
            for hm in head_masks:
                a_m = jnp.where(hm, a_t, 0.0)
                r_m = jnp.where(hm, r_t, 0.0)
                a_ab = jnp.where(strict, _dot_nt(a_m, b_t), 0.0)
                a_ak = jnp.where(strict, _dot_nt(a_m, k_t), 0.0)
                a_rb = jnp.where(incl, _dot_nt(r_m, b_t), 0.0)
                a_rk = jnp.where(incl, _dot_nt(r_m, k_t), 0.0)
                t_inv = _unit_lower_inverse(a_ab, L)
                rhs = _dot_nt(a_m, s_st) + _dot(a_ak, v)
                u = jnp.where(hm, _dot_hi(t_inv, rhs), u)
                intra.append((a_rb, a_rk))
            y = _dot_nt(r_t, s_st)
            for hm, (a_rb, a_rk) in zip(head_masks, intra):
                y = y + jnp.where(hm, _dot(a_rb, u) + _dot(a_rk, v), 0.0)
            w_last = jnp.exp(cum[L - 1:L, :])
            s_new = (s_st + _dot_tn(u, b_t) + _dot_tn(v, k_t)) * w_last
            s_ref[b, p] = jnp.where(same_head, s_new, 0.0)

            mean = seg_sum(y) * (1.0 / R_DH)
            yc = y - mean
            var = seg_sum(yc * yc) * (1.0 / R_DH)
            yn = yc * lax.rsqrt(var + GN_EPS) * lnw_ref[:, sl] + lnb_ref[:, sl]
            bonus = seg_sum(r * k * rk_ref[:, sl]) * v
            y_ref[b, :, sl] = _bf((yn + bonus) * g_all[:, sl])


def _rwkv(proj3, mu, wwa, w0, a0, g2, k_k, k_a, r_k, ln_w, ln_b, s0, sh0):
    B, T, _ = proj3.shape
    L = min(CHUNK, T)
    bb = 2
    NP = R_HEADS // 2
    vec = lambda n: pl.BlockSpec((1, n), lambda b, t: (0, 0))
    return pl.pallas_call(
        functools.partial(_rwkv_body, bb=bb, L=L),
        grid=(B // bb, T // L),
        in_specs=[pl.BlockSpec((bb, L, 4 * W_GRP), lambda b, t: (b, t, COL_R // (4 * W_GRP))),
                  vec(N_R),
                  pl.BlockSpec((LANES, 2 * W_GRP), lambda b, t: (0, 0)),
                  vec(W_GRP), vec(W_GRP),
                  pl.BlockSpec((R_LR_G, W_GRP), lambda b, t: (0, 0)),
                  vec(W_GRP), vec(W_GRP), vec(W_GRP), vec(W_GRP), vec(W_GRP),
                  pl.BlockSpec((bb, NP, LANES, LANES), lambda b, t: (b, 0, 0, 0)),
                  pl.BlockSpec((bb, 1, N_R), lambda b, t: (b, 0, 0))],
        out_specs=[pl.BlockSpec((bb, L, W_GRP), lambda b, t: (b, t, 0)),
                   pl.BlockSpec((bb, NP, LANES, LANES), lambda b, t: (b, 0, 0, 0)),
                   pl.BlockSpec((bb, 1, N_R), lambda b, t: (b, 0, 0))],
        out_shape=[jax.ShapeDtypeStruct((B, T, W_GRP), BF16),
                   jax.ShapeDtypeStruct((B, NP, LANES, LANES), F32),
                   jax.ShapeDtypeStruct((B, 1, N_R), F32)],
        scratch_shapes=[pltpu.VMEM((bb, 8 + L, N_R), F32)],
        compiler_params=_params(("arbitrary", "arbitrary")),
        name="rwkv7",
    )(proj3, mu, wwa, w0, a0, g2, k_k, k_a, r_k, ln_w, ln_b, s0, sh0)


def _s5_body(u_ref, win_ref, wout_ref, lam_ref, pw_ref, d_ref, wglu_ref, bglu_ref, hr0_ref, hi0_ref,
             y_ref, hr_ref, hi_ref, sr_ref, si_ref, *, Lb):
    @pl.when(pl.program_id(1) == 0)
    def _():
        hr_ref[...] = hr0_ref[...]
        hi_ref[...] = hi0_ref[...]

    NB = S5_N // W_GRP
    u = u_ref[0]
    for c in range(NB):
        bu = _dot(u[:, c * LANES:(c + 1) * LANES], win_ref[c])
        sr_ref[:, c * W_GRP:(c + 1) * W_GRP] = bu[:, 0:W_GRP]
        si_ref[:, c * W_GRP:(c + 1) * W_GRP] = bu[:, W_GRP:2 * W_GRP]

    row8 = _iota2((8, S5_N), 0)
    lam_r = [lam_ref[k:k + 1, :] for k in range(3)]
    lam_i = [lam_ref[3 + k:4 + k, :] for k in range(3)]
    pw_r = pw_ref[0:8, :]
    pw_i = pw_ref[8:16, :]

    def tile(n, carry):
        cr, ci = carry
        base = pl.multiple_of(n * 8, 8)
        xr = sr_ref[pl.ds(base, 8), :]
        xi = si_ref[pl.ds(base, 8), :]
        for lvl, sft in enumerate((1, 2, 4)):
            keep = row8 >= sft
            zr = jnp.where(keep, pltpu.roll(xr, sft, axis=0), 0.0)
            zi = jnp.where(keep, pltpu.roll(xi, sft, axis=0), 0.0)
            xr, xi = (xr + lam_r[lvl] * zr - lam_i[lvl] * zi,
                      xi + lam_r[lvl] * zi + lam_i[lvl] * zr)
        xr, xi = xr + pw_r * cr - pw_i * ci, xi + pw_r * ci + pw_i * cr
        sr_ref[pl.ds(base, 8), :] = xr
        si_ref[pl.ds(base, 8), :] = xi
        return xr[7:8, :], xi[7:8, :]

    cr, ci = lax.fori_loop(0, Lb // 8, tile, (hr_ref[0], hi_ref[0]))
    hr_ref[0] = cr
    hi_ref[0] = ci

    ys = []
    for c in range(NB):
        sl = slice(c * W_GRP, (c + 1) * W_GRP)
        ys.append(_dot(sr_ref[:, sl], wout_ref[c, 0]) + _dot(si_ref[:, sl], wout_ref[c, 1]))
    y = jnp.concatenate(ys, axis=-1) + d_ref[...] * u
    y = _gelu_tanh(y)
    y_ref[0] = _bf(y * _sigmoid(_dot(y, wglu_ref[...]) + bglu_ref[...]))


def _s5(proj3, win, wout, lam_pows, row_pows, d_vec, w_glu, b_glu, hr0, hi0):
    B, T, _ = proj3.shape
    Lb = min(512, T)
    st = pl.BlockSpec((1, 1, S5_N), lambda b, t: (b, 0, 0))
    full = lambda a: pl.BlockSpec(a.shape, lambda b, t: (0,) * a.ndim)
    return pl.pallas_call(
        functools.partial(_s5_body, Lb=Lb),
        grid=(B, T // Lb),
        in_specs=[pl.BlockSpec((1, Lb, W_GRP), lambda b, t: (b, t, COL_S // W_GRP)),
                  full(win), full(wout), full(lam_pows), full(row_pows), full(d_vec),
                  full(w_glu), full(b_glu), st, st],
        out_specs=[pl.BlockSpec((1, Lb, W_GRP), lambda b, t: (b, t, 0)), st, st],
        out_shape=[jax.ShapeDtypeStruct((B, T, W_GRP), BF16),
                   jax.ShapeDtypeStruct((B, 1, S5_N), F32),
                   jax.ShapeDtypeStruct((B, 1, S5_N), F32)],
        scratch_shapes=[pltpu.VMEM((Lb, S5_N), F32), pltpu.VMEM((Lb, S5_N), F32)],
        compiler_params=_params(("arbitrary", "arbitrary")),
        name="s5",
    )(proj3, win, wout, lam_pows, row_pows, d_vec, w_glu, b_glu, hr0, hi0)


def _ffn_body(x_ref, ym_ref, ys_ref, yr_ref, yg_ref, wo_ref, gpm_ref, gpf_ref, wg_ref, wu_ref, wd_ref, gpo_ref,
              o_ref, hf_ref, acc_ref):
    f = pl.program_id(1)

    @pl.when(f == 0)
    def _():
        mix = jnp.dot(ym_ref[...], wo_ref[0], preferred_element_type=F32)
        mix = mix + jnp.dot(ys_ref[...], wo_ref[1], preferred_element_type=F32)
        mix = mix + jnp.dot(yr_ref[...], wo_ref[2], preferred_element_type=F32)
        mix = mix + jnp.dot(yg_ref[...], wo_ref[3], preferred_element_type=F32)
        x1 = x_ref[...] + _rms(mix, gpm_ref[...])
        o_ref[...] = x1
        hf_ref[...] = _bf(_rms(x1, gpf_ref[...]))
        acc_ref[...] = jnp.zeros_like(acc_ref)

    hf = hf_ref[...]
    gate = jnp.dot(hf, wg_ref[...], preferred_element_type=F32)
    up = jnp.dot(hf, wu_ref[...], preferred_element_type=F32)
    acc_ref[...] += jnp.dot(_bf(_silu(gate) * up), wd_ref[...], preferred_element_type=F32)

    @pl.when(f == pl.num_programs(1) - 1)
    def _():
        o_ref[...] = o_ref[...] + _rms(acc_ref[...], gpo_ref[...])


def _out_ffn(x2, ym, ys, yr, yg, w_out, g_post_mix, g_pre_ffn, w_gate, w_up, w_down, g_post_ffn):
    n = x2.shape[0]
    tm = min(512, n)
    tf = 512
    row = lambda w: pl.BlockSpec((tm, w), lambda i, f: (i, 0))
    vec = pl.BlockSpec((1, D_MODEL), lambda i, f: (0, 0))
    return pl.pallas_call(
        _ffn_body,
        grid=(n // tm, D_FF // tf),
        in_specs=[row(D_MODEL), row(W_GRP), row(W_GRP), row(W_GRP), row(W_GRP),
                  pl.BlockSpec((4, W_GRP, D_MODEL), lambda i, f: (0, 0, 0), pipeline_mode=pl.Buffered(1)),
                  vec, vec,
                  pl.BlockSpec((D_MODEL, tf), lambda i, f: (0, f)),
                  pl.BlockSpec((D_MODEL, tf), lambda i, f: (0, f)),
                  pl.BlockSpec((tf, D_MODEL), lambda i, f: (f, 0)),
                  vec],
        out_specs=row(D_MODEL),
        out_shape=jax.ShapeDtypeStruct((n, D_MODEL), F32),
        scratch_shapes=[pltpu.VMEM((tm, D_MODEL), BF16), pltpu.VMEM((tm, D_MODEL), F32)],
        compiler_params=_params(("arbitrary", "arbitrary")),
        name="out_ffn",
    )(x2, ym, ys, yr, yg, w_out, g_post_mix, g_pre_ffn, w_gate, w_up, w_down, g_post_ffn)


def _lane_row(pairs):
    row = jnp.zeros((LANES,), F32)
    for off, vals in pairs:
        row = lax.dynamic_update_slice(row, vals.astype(F32), (off,))
    return row[None, :]


def _block_diag(blocks):
    n, r, c = blocks.shape
    eye = jnp.eye(n, dtype=blocks.dtype)
    return (eye[:, None, :, None] * blocks[:, :, None, :]).reshape(n * r, n * c)


def _complex_pow(re, im, n):
    pr, pi = re, im
    for _ in range(n - 1):
        pr, pi = pr * re - pi * im, pr * im + pi * re
    return pr, pi


def _prep_layer(p):
    f32 = lambda a: a.astype(F32)
    w_in = p['w_in']
    off_s = N_M
    off_r = N_M + N_S
    off_g = N_M + N_S + N_R
    gate_cols = jnp.concatenate([w_in[:, 4 * W_GRP:N_M], w_in[:, off_g + 4 * W_GRP:off_g + N_G]], axis=1)
    w_perm = jnp.concatenate([
        w_in[:, 0:4 * W_GRP],
        w_in[:, off_g:off_g + 4 * W_GRP],
        w_in[:, off_r:off_r + N_R],
        gate_cols,
        jnp.zeros((D_MODEL, COL_S - COL_GATE - 16), w_in.dtype),
        w_in[:, off_s:off_s + N_S]], axis=1)
    q = {'w_in': _bf(w_perm)}
    q['g_pre_mix'] = f32(p['g_pre_mix'])[None, :]

    gb = f32(p['mlstm_gate_bias'])
    q['m_gate_bias'] = _lane_row([(GATE_MI, gb[0]), (GATE_MF, gb[1])])
    q['m_norm_g'] = f32(p['mlstm_norm_g'])[None, :]

    lam_re, lam_im = f32(p['s5_lam_re']), f32(p['s5_lam_im'])
    dt = jnp.exp(f32(p['s5_log_dt']))[:, None]
    mag = jnp.exp(lam_re * dt)
    lb_re = mag * jnp.cos(lam_im * dt)
    lb_im = mag * jnp.sin(lam_im * dt)
    nr = lb_re - 1.0
    den = lam_re * lam_re + lam_im * lam_im
    f_re = (nr * lam_re + lb_im * lam_im) / den
    f_im = (lb_im * lam_re - nr * lam_im) / den
    B_re, B_im = f32(p['s5_B_re']), f32(p['s5_B_im'])
    Bb_re = f_re[..., None] * B_re - f_im[..., None] * B_im
    Bb_im = f_re[..., None] * B_im + f_im[..., None] * B_re
    nb, gpb = S5_N // W_GRP, S5_GROUPS // (S5_N // W_GRP)
    bd_in = lambda m: jnp.stack([_block_diag(jnp.swapaxes(m, 1, 2)[c * gpb:(c + 1) * gpb]) for c in range(nb)])
    q['s5_win'] = _bf(jnp.concatenate([bd_in(Bb_re), bd_in(Bb_im)], axis=-1))
    bd_out = lambda m: jnp.stack([_block_diag(jnp.swapaxes(m, 1, 2)[c * gpb:(c + 1) * gpb]) for c in range(nb)])
    q['s5_wout'] = _bf(jnp.stack([bd_out(f32(p['s5_C_re'])), -bd_out(f32(p['s5_C_im']))], axis=1))
    lr, li = lb_re.reshape(1, S5_N), lb_im.reshape(1, S5_N)
    pows = [_complex_pow(lr, li, n) for n in range(1, 9)]
    zero = jnp.zeros((2, S5_N), F32)
    q['s5_lam_pows'] = jnp.concatenate([pows[0][0], pows[1][0], pows[3][0], pows[0][1], pows[1][1], pows[3][1], zero], 0)
    q['s5_row_pows'] = jnp.concatenate([pw[0] for pw in pows] + [pw[1] for pw in pows], 0)
    q['s5_D'] = f32(p['s5_D'])[None, :]
    q['s5_w_glu'] = _bf(p['s5_w_glu'])
    q['s5_b_glu'] = f32(p['s5_b_glu'])[None, :]

    q['r_mu'] = f32(p['rwkv_mu'])[None, :]
    zw = jnp.zeros((R_LR_W, W_GRP), F32)
    q['r_wwa'] = _bf(jnp.concatenate([jnp.concatenate([f32(p['rwkv_w2']), zw], 1),
                                      jnp.concatenate([zw, f32(p['rwkv_a2'])], 1)], 0))
    q['r_w0'] = f32(p['rwkv_w0'])[None, :]
    q['r_a0'] = f32(p['rwkv_a0'])[None, :]
    q['r_g2'] = _bf(p['rwkv_g2'])
    q['r_k_k'] = f32(p['rwkv_k_k'])[None, :]
    q['r_k_a'] = f32(p['rwkv_k_a'])[None, :]
    q['r_r_k'] = f32(p['rwkv_r_k']).reshape(1, W_GRP)
    q['r_ln_w'] = f32(p['rwkv_ln_w'])[None, :]
    q['r_ln_b'] = f32(p['rwkv_ln_b'])[None, :]

    q['g_alog'] = _lane_row([(GATE_GA, f32(p['gdn_A_log']))])
    q['g_dtb'] = _lane_row([(GATE_GA, f32(p['gdn_dt_bias']))])
    q['g_conv_w'] = f32(p['gdn_conv_w'])
    q['g_norm_g'] = f32(p['gdn_norm_g'])[None, :]

    q['w_out'] = _bf(p['w_out']).reshape(4, W_GRP, D_MODEL)
    q['g_post_mix'] = f32(p['g_post_mix'])[None, :]
    q['g_pre_ffn'] = f32(p['g_pre_ffn'])[None, :]
    q['w_gate'] = _bf(p['w_gate'])
    q['w_up'] = _bf(p['w_up'])
    q['w_down'] = _bf(p['w_down'])
    q['g_post_ffn'] = f32(p['g_post_ffn'])[None, :]
    return q


def _pack_rwkv_state(s):
    B = s.shape[0]
    s = s.reshape(B, R_HEADS // 2, 2, R_DH, R_DH)
    z = jnp.zeros_like(s[:, :, 0])
    top = jnp.concatenate([s[:, :, 0], z], axis=-1)
    bot = jnp.concatenate([z, s[:, :, 1]], axis=-1)
    return jnp.concatenate([top, bot], axis=-2)


def _unpack_rwkv_state(s):
    B = s.shape[0]
    a = s[:, :, :R_DH, :R_DH]
    d = s[:, :, R_DH:, R_DH:]
    return jnp.stack([a, d], axis=2).reshape(B, R_HEADS, R_DH, R_DH)


def _layer(x, q, state):
    mC0, mn0, mm0, s5r0, s5i0, rS0, rsh0, gS0, gcv0 = state
    B, T, _ = x.shape
    x2 = x.reshape(B * T, D_MODEL)
    proj3 = _proj(x2, q['g_pre_mix'], q['w_in']).reshape(B, T, N_PROJ)

    mm0p = jnp.pad(mm0, ((0, 0), (0, LANES - M_HEADS)))[:, None, :]
    y_m, mC, mn, mmp = _mlstm(proj3, q['m_gate_bias'], q['m_norm_g'], mC0, mn0, mm0p)
    mm = mmp[:, 0, :M_HEADS]

    y_s, s5r, s5i = _s5(proj3, q['s5_win'], q['s5_wout'], q['s5_lam_pows'], q['s5_row_pows'], q['s5_D'],
                        q['s5_w_glu'], q['s5_b_glu'],
                        s5r0.reshape(B, 1, S5_N), s5i0.reshape(B, 1, S5_N))
    s5r = s5r.reshape(B, S5_GROUPS, S5_P)
    s5i = s5i.reshape(B, S5_GROUPS, S5_P)

    y_r, rSp, rsh = _rwkv(proj3, q['r_mu'], q['r_wwa'], q['r_w0'], q['r_a0'], q['r_g2'], q['r_k_k'], q['r_k_a'],
                          q['r_r_k'], q['r_ln_w'], q['r_ln_b'], _pack_rwkv_state(rS0), rsh0[:, None, :])
    rS = _unpack_rwkv_state(rSp)
    rsh = rsh[:, 0, :]

    y_g, gS, gcv = _gdn(proj3, q['g_alog'], q['g_dtb'], q['g_conv_w'], q['g_norm_g'], gS0, gcv0)

    w = lambda a: a.reshape(B * T, W_GRP)
    out = _out_ffn(x2, w(y_m), w(y_s), w(y_r), w(y_g), q['w_out'], q['g_post_mix'], q['g_pre_ffn'],
                   q['w_gate'], q['w_up'], q['w_down'], q['g_post_ffn'])
    return out.reshape(B, T, D_MODEL), (mC, mn, mm, s5r, s5i, rS, rsh, gS, gcv)


def _zero_state(b):
    z = lambda *s: jnp.zeros(s, F32)
    return (z(b, M_HEADS, M_DH, M_DH), z(b, M_HEADS, M_DH), z(b, M_HEADS),
            z(b, S5_GROUPS, S5_P), z(b, S5_GROUPS, S5_P),
            z(b, R_HEADS, R_DH, R_DH), z(b, N_R),
            z(b, G_HEADS, G_DH, G_DH), z(b, CONV_W - 1, 3 * W_GRP))


_PARAM_NAMES = ('g_pre_mix', 'w_in', 'mlstm_gate_bias', 'mlstm_norm_g',
                's5_lam_re', 's5_lam_im', 's5_log_dt', 's5_B_re', 's5_B_im', 's5_C_re', 's5_C_im',
                's5_D', 's5_w_glu', 's5_b_glu',
                'rwkv_mu', 'rwkv_w0', 'rwkv_w2', 'rwkv_a0', 'rwkv_a2', 'rwkv_g2', 'rwkv_k_k', 'rwkv_k_a',
                'rwkv_r_k', 'rwkv_ln_w', 'rwkv_ln_b',
                'gdn_conv_w', 'gdn_A_log', 'gdn_dt_bias', 'gdn_norm_g',
                'w_out', 'g_post_mix', 'g_pre_ffn', 'w_gate', 'w_up', 'w_down', 'g_post_ffn')


def kernel(x_prompt, x_sample, state_mlstm_C, state_mlstm_n, state_mlstm_m, state_s5_re, state_s5_im, state_rwkv_S, state_rwkv_shift, state_gdn_S, state_gdn_conv, g_pre_mix, w_in, mlstm_gate_bias, mlstm_norm_g, s5_lam_re, s5_lam_im, s5_log_dt, s5_B_re, s5_B_im, s5_C_re, s5_C_im, s5_D, s5_w_glu, s5_b_glu, rwkv_mu, rwkv_w0, rwkv_w2, rwkv_a0, rwkv_a2, rwkv_g2, rwkv_k_k, rwkv_k_a, rwkv_r_k, rwkv_ln_w, rwkv_ln_b, gdn_conv_w, gdn_A_log, gdn_dt_bias, gdn_norm_g, w_out, g_post_mix, g_pre_ffn, w_gate, w_up, w_down, g_post_ffn):
    weights = (g_pre_mix, w_in, mlstm_gate_bias, mlstm_norm_g,
               s5_lam_re, s5_lam_im, s5_log_dt, s5_B_re, s5_B_im, s5_C_re, s5_C_im, s5_D, s5_w_glu, s5_b_glu,
               rwkv_mu, rwkv_w0, rwkv_w2, rwkv_a0, rwkv_a2, rwkv_g2, rwkv_k_k, rwkv_k_a, rwkv_r_k,
               rwkv_ln_w, rwkv_ln_b, gdn_conv_w, gdn_A_log, gdn_dt_bias, gdn_norm_g,
               w_out, g_post_mix, g_pre_ffn, w_gate, w_up, w_down, g_post_ffn)
    caches = (state_mlstm_C, state_mlstm_n, state_mlstm_m, state_s5_re, state_s5_im,
              state_rwkv_S, state_rwkv_shift, state_gdn_S, state_gdn_conv)
    yp, ys = x_prompt, x_sample
    outs_p, outs_s = [], []
    for l in range(DEPTH):
        q = _prep_layer({name: wt[l] for name, wt in zip(_PARAM_NAMES, weights)})
        yp, st_p = _layer(yp, q, _zero_state(x_prompt.shape[0]))
        ys, st_s = _layer(ys, q, tuple(c[l].astype(F32) for c in caches))
        outs_p.append(st_p)
        outs_s.append(st_s)
    stack = lambda outs: [jnp.stack(t) for t in zip(*outs)]
    return (yp, ys, *stack(outs_p), *stack(outs_s))
```

```python
import functools
import math

import jax
import jax.numpy as jnp
from jax import lax
from jax.experimental import pallas as pl
from jax.experimental.pallas import tpu as pltpu

F32 = jnp.float32
BF16 = jnp.bfloat16

D_MODEL = 2048
DEPTH = 2
EPS = 1e-6
GN_EPS = 64e-5
W_GRP = D_MODEL // 4
M_HEADS, M_DH = 4, 128
S5_CH, S5_GROUPS, S5_P = 16, 32, 64
S5_N = S5_GROUPS * S5_P
R_DH, R_HEADS = 64, 8
R_LR_W, R_LR_A, R_LR_G = 64, 64, 128
G_HEADS, G_DH = 4, 128
CONV_W = 4
D_FF = 5632
N_M = 4 * W_GRP + 2 * M_HEADS
N_S = W_GRP
N_R = 3 * W_GRP + R_LR_W + R_LR_A + R_LR_G
N_G = 4 * W_GRP + 2 * G_HEADS

COL_M = 0
COL_G = 2048
COL_R = 4096
COL_GATE = COL_R + N_R
COL_S = 6144
N_PROJ = 6656
GATE_MI, GATE_MF, GATE_GA, GATE_GB = 0, 4, 8, 12

CHUNK = 64
LANES = 128
VMEM_LIMIT = 56 * 1024 * 1024


def _bf(x):
    return x.astype(BF16)


def _dot(a, b):
    return jnp.dot(_bf(a), _bf(b), preferred_element_type=F32)


def _dot_nt(a, b):
    return lax.dot_general(_bf(a), _bf(b), (((1,), (1,)), ((), ())), preferred_element_type=F32)


def _dot_tn(a, b):
    return lax.dot_general(_bf(a), _bf(b), (((0,), (0,)), ((), ())), preferred_element_type=F32)


def _split3(x):
    hi = _bf(x)
    r1 = x - hi.astype(F32)
    mid = _bf(r1)
    lo = _bf(r1 - mid.astype(F32))
    return hi, mid, lo


def _dot_exact_rhs(a_bf, x):
    hi, mid, lo = _split3(x)
    d = lambda v: jnp.dot(a_bf, v, preferred_element_type=F32)
    return d(hi) + d(mid) + d(lo)


def _dot_exact_lhs(x, b_bf):
    hi, mid, lo = _split3(x)
    d = lambda v: jnp.dot(v, b_bf, preferred_element_type=F32)
    return d(hi) + d(mid) + d(lo)


def _dot_nt_exact_rhs(a_bf, x):
    hi, mid, lo = _split3(x)
    d = lambda v: lax.dot_general(a_bf, v, (((1,), (1,)), ((), ())), preferred_element_type=F32)
    return d(hi) + d(mid) + d(lo)


def _sigmoid(x):
    return 1.0 / (1.0 + jnp.exp(-x))


def _softplus(x):
    return jnp.maximum(x, 0.0) + jnp.log(1.0 + jnp.exp(-jnp.abs(x)))


def _log_sigmoid(x):
    return -_softplus(-x)


def _silu(x):
    return x * _sigmoid(x)


def _gelu_tanh(x):
    return 0.5 * x * (1.0 + jnp.tanh(math.sqrt(2.0 / math.pi) * (x + 0.044715 * (x * x * x))))


def _rms(x, g):
    return x * lax.rsqrt(jnp.mean(x * x, -1, keepdims=True) + EPS) * g


def _iota2(shape, axis):
    return lax.broadcasted_iota(jnp.int32, shape, axis)


def _tri_masks(n):
    r, c = _iota2((n, n), 0), _iota2((n, n), 1)
    return r >= c, r > c


def _ones_where(mask):
    return jnp.where(mask, 1.0, 0.0).astype(BF16)


def _neumann_tail(n_mat, order):
    tm = n_mat
    p = n_mat
    k = 1
    while 2 * k < order:
        p = _dot(p, p)
        tm = tm + p + _dot(tm, p)
        k *= 2
    return tm


def _block_masks(n_blocks, L):
    n = n_blocks * L
    r, c = _iota2((n, n), 0), _iota2((n, n), 1)
    blk = lambda i: sum(jnp.where(i >= j * L, 1, 0) for j in range(1, n_blocks))
    same = blk(r) == blk(c)
    return same & (r >= c), same & (r > c)


def _col(x, j):
    return x[:, j:j + 1]


def _row(x, i):
    return x[i:i + 1, :]


def _params(sem):
    return pltpu.CompilerParams(dimension_semantics=sem, vmem_limit_bytes=VMEM_LIMIT)


def _proj_body(x_ref, g_ref, w_ref, o_ref, h_ref):
    @pl.when(pl.program_id(1) == 0)
    def _():
        h_ref[...] = _bf(_rms(x_ref[...], g_ref[...]))
    o_ref[...] = jnp.dot(h_ref[...], w_ref[...], preferred_element_type=F32)


def _proj(x2, g, w):
    n = x2.shape[0]
    tm = min(512, n)
    tn = N_PROJ // 4
    return pl.pallas_call(
        _proj_body,
        grid=(n // tm, N_PROJ // tn),
        in_specs=[pl.BlockSpec((tm, D_MODEL), lambda i, j: (i, 0)),
                  pl.BlockSpec((1, D_MODEL), lambda i, j: (0, 0)),
                  pl.BlockSpec((D_MODEL, tn), lambda i, j: (0, j))],
        out_specs=pl.BlockSpec((tm, tn), lambda i, j: (i, j)),
        out_shape=jax.ShapeDtypeStruct((n, N_PROJ), F32),
        scratch_shapes=[pltpu.VMEM((tm, D_MODEL), BF16)],
        compiler_params=_params(("arbitrary", "arbitrary")),
        name="in_proj",
    )(x2, g, w)


def _mlstm_body(p_ref, gt_ref, gb_ref, ng_ref, c0_ref, n0_ref, m0_ref,
                y_ref, c_ref, n_ref, m_ref, *, bb, L):
    @pl.when(pl.program_id(1) == 0)
    def _():
        c_ref[...] = c0_ref[...]
        n_ref[...] = n0_ref[...]
        m_ref[...] = m0_ref[...]

    incl, _ = _tri_masks(L)
    tri_lo = _ones_where(incl)
    tri_up = _ones_where(_iota2((L, L), 0) <= _iota2((L, L), 1))
    sel = _ones_where(_iota2((8, LANES), 0) == _iota2((8, LANES), 1))
    lane = _iota2((1, LANES), 1)
    scale = M_DH ** -0.5

    for b in range(bb):
        gates = gt_ref[b] + gb_ref[...]
        lf_cols = _log_sigmoid(gates)
        b_cols = _dot_exact_rhs(tri_lo, lf_cols)
        rows = _dot_nt_exact_rhs(sel, jnp.where(lane < GATE_MF, gates, lf_cols))
        b_rows = _dot_exact_lhs(rows, tri_up)
        m_row = m_ref[b]
        m_out = m_row
        for h in range(M_HEADS):
            sl = slice(h * M_DH, (h + 1) * M_DH)
            q = p_ref[b, :, sl] * scale
            k = p_ref[b, :, W_GRP + h * M_DH:W_GRP + (h + 1) * M_DH]
            v = p_ref[b, :, 2 * W_GRP + h * M_DH:2 * W_GRP + (h + 1) * M_DH]
            o = p_ref[b, :, 3 * W_GRP + h * M_DH:3 * W_GRP + (h + 1) * M_DH]
            li_c = _col(gates, GATE_MI + h)
            b_c = _col(b_cols, GATE_MF + h)
            li_r = _row(rows, GATE_MI + h)
            b_r = _row(b_rows, GATE_MF + h)
            m_prev = jnp.sum(jnp.where(lane == h, m_row, 0.0), axis=1, keepdims=True)
            c_st = c_ref[b, h]
            n_st = n_ref[b, h:h + 1, :]

            log_d = jnp.where(incl, b_c - b_r + li_r, -jnp.inf)
            inter = b_c + m_prev
            mt = jnp.maximum(inter, jnp.max(log_d, axis=-1, keepdims=True))
            s = _dot_nt(q, k) * jnp.exp(log_d - mt)
            w_c = jnp.exp(inter - mt)
            num = _dot(s, v) + _dot(q, c_st) * w_c
            den = jnp.sum(s, -1, keepdims=True) + w_c * jnp.sum(q * n_st, -1, keepdims=True)
            hh = num / jnp.maximum(jnp.abs(den), jnp.exp(-mt))

            b_last = b_c[L - 1:L, :]
            tail = b_last - b_c + li_c
            m_new = jnp.maximum(b_last + m_prev, jnp.max(tail, axis=0, keepdims=True))
            wk = jnp.exp(tail - m_new)
            sc = jnp.exp(b_last + m_prev - m_new)
            c_ref[b, h] = sc * c_st + _dot_tn(k, wk * v)
            n_ref[b, h:h + 1, :] = sc * n_st + jnp.sum(wk * k, axis=0, keepdims=True)
            m_out = jnp.where(lane == h, m_new, m_out)

            mu = jnp.mean(hh, -1, keepdims=True)
            hc = hh - mu
            hn = hc * lax.rsqrt(jnp.mean(hc * hc, -1, keepdims=True) + EPS)
            y_ref[b, :, sl] = _bf(_sigmoid(o) * (hn * ng_ref[:, sl]))
        m_ref[b] = m_out


def _mlstm(proj3, gate_bias, norm_g, c0, n0, m0):
    B, T, _ = proj3.shape
    L = min(CHUNK, T)
    bb = 2
    st = lambda b, t: (b, 0, 0)
    return pl.pallas_call(
        functools.partial(_mlstm_body, bb=bb, L=L),
        grid=(B // bb, T // L),
        in_specs=[pl.BlockSpec((bb, L, 4 * W_GRP), lambda b, t: (b, t, COL_M // (4 * W_GRP))),
                  pl.BlockSpec((bb, L, LANES), lambda b, t: (b, t, COL_GATE // LANES)),
                  pl.BlockSpec((1, LANES), lambda b, t: (0, 0)),
                  pl.BlockSpec((1, W_GRP), lambda b, t: (0, 0)),
                  pl.BlockSpec((bb, M_HEADS, M_DH, M_DH), lambda b, t: (b, 0, 0, 0)),
                  pl.BlockSpec((bb, M_HEADS, M_DH), st),
                  pl.BlockSpec((bb, 1, LANES), st)],
        out_specs=[pl.BlockSpec((bb, L, W_GRP), lambda b, t: (b, t, 0)),
                   pl.BlockSpec((bb, M_HEADS, M_DH, M_DH), lambda b, t: (b, 0, 0, 0)),
                   pl.BlockSpec((bb, M_HEADS, M_DH), st),
                   pl.BlockSpec((bb, 1, LANES), st)],
        out_shape=[jax.ShapeDtypeStruct((B, T, W_GRP), BF16),
                   jax.ShapeDtypeStruct((B, M_HEADS, M_DH, M_DH), F32),
                   jax.ShapeDtypeStruct((B, M_HEADS, M_DH), F32),
                   jax.ShapeDtypeStruct((B, 1, LANES), F32)],
        compiler_params=_params(("arbitrary", "arbitrary")),
        name="mlstm",
    )(proj3, proj3, gate_bias, norm_g, c0, n0, m0)


def _gdn_body(p_ref, gt_ref, alog_ref, dtb_ref, cw_ref, ng_ref, s0_ref, cv0_ref,
              y_ref, s_ref, cv_ref, xp_ref, *, bb, L):
    t = pl.program_id(1)
    W3 = 3 * W_GRP
    PAD = 8
    H = G_HEADS

    @pl.when(t == 0)
    def _():
        s_ref[...] = s0_ref[...]
        xp_ref[:, PAD - (CONV_W - 1):PAD, :] = cv0_ref[...]

    incl, _ = _tri_masks(L)
    tri_lo = _ones_where(incl)
    incl_bd, strict_bd = _block_masks(H, L)
    sel = _ones_where(_iota2((8, LANES), 0) == _iota2((8, LANES), 1))
    lane = _iota2((1, LANES), 1)
    neg_a = -jnp.exp(alog_ref[...])
    rows_of = lambda x, h: x[h * L:(h + 1) * L]
    stack = lambda xs: jnp.concatenate(xs, axis=0)

    for b in range(bb):
        xp_ref[b, PAD:PAD + L, :] = p_ref[b, :, 0:W3]
        acc = xp_ref[b, PAD - 3:PAD - 3 + L, :] * cw_ref[0:1, :]
        for j in range(1, CONV_W):
            acc = acc + xp_ref[b, PAD - 3 + j:PAD - 3 + j + L, :] * cw_ref[j:j + 1, :]
        tail3 = xp_ref[b, PAD + L - (CONV_W - 1):PAD + L, :]
        xp_ref[b, PAD - (CONV_W - 1):PAD, :] = tail3
        cv_ref[b] = tail3

        acc = _silu(acc)
        gates = gt_ref[b]
        g_cols = neg_a * _softplus(gates + dtb_ref[...])
        beta_cols = _sigmoid(gates)
        gc_cols = _dot_exact_rhs(tri_lo, g_cols)

        qs, ks, vs = [], [], []
        for h in range(H):
            q = acc[:, h * G_DH:(h + 1) * G_DH]
            k = acc[:, W_GRP + h * G_DH:W_GRP + (h + 1) * G_DH]
            qs.append(q * lax.rsqrt(jnp.sum(q * q, -1, keepdims=True) + 1e-6) * (G_DH ** -0.5))
            ks.append(k * lax.rsqrt(jnp.sum(k * k, -1, keepdims=True) + 1e-6))
            vs.append(acc[:, 2 * W_GRP + h * G_DH:2 * W_GRP + (h + 1) * G_DH])
        q_s, k_s, v_s = stack(qs), stack(ks), stack(vs)
        g_c = stack([_col(gc_cols, GATE_GA + h) for h in range(H)])
        beta = stack([_col(beta_cols, GATE_GB + h) for h in range(H)])
        g_r = _dot_nt_exact_rhs(sel, jnp.where(lane == 0, g_c, 0.0))[0:1, :]
        s_st = [s_ref[b, h] for h in range(H)]

        diff = g_c - g_r
        eg = jnp.exp(g_c)
        a_mat = beta * _dot_nt(k_s, k_s) * jnp.exp(jnp.where(strict_bd, diff, -jnp.inf))
        tm = _neumann_tail(-a_mat, L)
        rhs = jnp.concatenate([beta * v_s, (beta * eg) * k_s], axis=1)
        sol = rhs + _dot(tm, rhs)
        u = sol[:, 0:G_DH] - stack([_dot(rows_of(sol, h)[:, G_DH:2 * G_DH], s_st[h]) for h in range(H)])
        qk = _dot_nt(q_s, k_s) * jnp.exp(jnp.where(incl_bd, diff, -jnp.inf))
        o_s = eg * stack([_dot(qs[h], s_st[h]) for h in range(H)]) + _dot(qk, u)

        for h in range(H):
            sl = slice(h * G_DH, (h + 1) * G_DH)
            g_ch = rows_of(g_c, h)
            g_last = g_ch[L - 1:L, :]
            s_ref[b, h] = jnp.exp(g_last) * s_st[h] + _dot_tn(ks[h] * jnp.exp(g_last - g_ch), rows_of(u, h))
            o = rows_of(o_s, h)
            o = o * lax.rsqrt(jnp.mean(o * o, -1, keepdims=True) + EPS) * ng_ref[...]
            z = p_ref[b, :, W3 + h * G_DH:W3 + (h + 1) * G_DH]
            y_ref[b, :, sl] = _bf(o * _silu(z))


def _gdn(proj3, alog_row, dtb_row, conv_w, norm_g, s0, cv0):
    B, T, _ = proj3.shape
    L = min(CHUNK, T)
    bb = 2
    return pl.pallas_call(
        functools.partial(_gdn_body, bb=bb, L=L),
        grid=(B // bb, T // L),
        in_specs=[pl.BlockSpec((bb, L, 4 * W_GRP), lambda b, t: (b, t, COL_G // (4 * W_GRP))),
                  pl.BlockSpec((bb, L, LANES), lambda b, t: (b, t, COL_GATE // LANES)),
                  pl.BlockSpec((1, LANES), lambda b, t: (0, 0)),
                  pl.BlockSpec((1, LANES), lambda b, t: (0, 0)),
                  pl.BlockSpec((CONV_W, 3 * W_GRP), lambda b, t: (0, 0)),
                  pl.BlockSpec((1, G_DH), lambda b, t: (0, 0)),
                  pl.BlockSpec((bb, G_HEADS, G_DH, G_DH), lambda b, t: (b, 0, 0, 0)),
                  pl.BlockSpec((bb, CONV_W - 1, 3 * W_GRP), lambda b, t: (b, 0, 0))],
        out_specs=[pl.BlockSpec((bb, L, W_GRP), lambda b, t: (b, t, 0)),
                   pl.BlockSpec((bb, G_HEADS, G_DH, G_DH), lambda b, t: (b, 0, 0, 0)),
                   pl.BlockSpec((bb, CONV_W - 1, 3 * W_GRP), lambda b, t: (b, 0, 0))],
        out_shape=[jax.ShapeDtypeStruct((B, T, W_GRP), BF16),
                   jax.ShapeDtypeStruct((B, G_HEADS, G_DH, G_DH), F32),
                   jax.ShapeDtypeStruct((B, CONV_W - 1, 3 * W_GRP), F32)],
        scratch_shapes=[pltpu.VMEM((bb, 8 + L, 3 * W_GRP), F32)],
        compiler_params=_params(("arbitrary", "arbitrary")),
        name="gdn",
    )(proj3, proj3, alog_row, dtb_row, conv_w, norm_g, s0, cv0)


def _rwkv_body(p_ref, mu_ref, wwa_ref, w0_ref, a0_ref, g2_ref, kk_ref, ka_ref, rk_ref,
               lnw_ref, lnb_ref, s0_ref, sh0_ref,
               y_ref, s_ref, sh_ref, xs_ref, *, bb, L):
    t = pl.program_id(1)
    PAD = 8
    NQ = 4

    @pl.when(t == 0)
    def _():
        s_ref[...] = s0_ref[...]
        xs_ref[:, PAD - 1:PAD, :] = sh0_ref[...]

    incl, _ = _tri_masks(L)
    tri_lo = _ones_where(incl)
    incl_bd, strict_bd = _block_masks(NQ, L)
    lane = _iota2((1, LANES), 1)
    head_masks = (lane < R_DH, lane >= R_DH)
    r_i, c_i = _iota2((LANES, LANES), 0), _iota2((LANES, LANES), 1)
    same_head = (r_i < R_DH) == (c_i < R_DH)
    seg = _ones_where(same_head)
    seg_sum = lambda x: _dot_exact_lhs(x, seg)
    rows_of = lambda x, i: x[i * L:(i + 1) * L]
    stack = lambda xs: jnp.concatenate(xs, axis=0)

    for b in range(bb):
        xs_ref[b, PAD:PAD + L, :] = p_ref[b, :, 0:N_R]
        cur = p_ref[b, :, 0:N_R]
        prev = xs_ref[b, PAD - 1:PAD - 1 + L, :]
        last = xs_ref[b, PAD + L - 1:PAD + L, :]
        xs_ref[b, PAD - 1:PAD, :] = last
        sh_ref[b] = last

        xm = cur + (prev - cur) * mu_ref[...]
        r_all = xm[:, 0:W_GRP]
        k_all = xm[:, W_GRP:2 * W_GRP]
        v_all = xm[:, 2 * W_GRP:3 * W_GRP]
        wa_code = xm[:, 3 * W_GRP:3 * W_GRP + LANES]
        g_code = xm[:, 3 * W_GRP + LANES:3 * W_GRP + 2 * LANES]
        wa_in = jnp.where(lane < R_LR_W, jnp.tanh(wa_code), wa_code)
        wa = _dot(wa_in, wwa_ref[...])
        w_all = -_softplus(-(w0_ref[...] + wa[:, 0:W_GRP])) - 0.5
        ld_all = -jnp.exp(w_all)
        a_all = _sigmoid(a0_ref[...] + wa[:, W_GRP:2 * W_GRP])
        g_all = _dot(_sigmoid(g_code), g2_ref[...])
        k2_all = k_all * (1.0 + (a_all - 1.0) * ka_ref[...])
        kkr_all = k_all * kk_ref[...]

        for qd in range(R_HEADS // NQ):
            pairs = (2 * qd, 2 * qd + 1)
            pr = {}
            for p in pairs:
                sl = slice(p * LANES, (p + 1) * LANES)
                r, k, v, ld, a, kkr = (r_all[:, sl], k2_all[:, sl], v_all[:, sl], ld_all[:, sl], a_all[:, sl],
                                       kkr_all[:, sl])
                kk = kkr * lax.rsqrt(seg_sum(kkr * kkr) + 1e-6)
                cum = _dot_exact_rhs(tri_lo, ld)
                e_neg = jnp.exp(-cum)
                pr[p] = dict(r=r, k=k, v=v, cum=cum, a_t=jnp.exp(cum - ld) * kk, b_t=-(kk * a) * e_neg,
                             k_t=k * e_neg, r_t=r * jnp.exp(cum), s=s_ref[b, p])
            slots = [(p, hm) for p in pairs for hm in head_masks]
            a_s = stack([jnp.where(hm, pr[p]['a_t'], 0.0) for p, hm in slots])
            r_s = stack([jnp.where(hm, pr[p]['r_t'], 0.0) for p, hm in slots])
            b_s = stack([pr[p]['b_t'] for p, _ in slots])
            k_s = stack([pr[p]['k_t'] for p, _ in slots])
            v_s = stack([pr[p]['v'] for p, _ in slots])
            a_ab = jnp.where(strict_bd, _dot_nt(a_s, b_s), 0.0)
            a_ak = jnp.where(strict_bd, _dot_nt(a_s, k_s), 0.0)
            a_rb = jnp.where(incl_bd, _dot_nt(r_s, b_s), 0.0)
            a_rk = jnp.where(incl_bd, _dot_nt(r_s, k_s), 0.0)
            tm = _neumann_tail(a_ab, L)
            rhs = stack([_dot_nt(rows_of(a_s, i), pr[p]['s']) for i, (p, _) in enumerate(slots)]) + _dot(a_ak, v_s)
            u_s = rhs + _dot(tm, rhs)
            y_s = _dot(a_rb, u_s) + _dot(a_rk, v_s)

            for j, p in enumerate(pairs):
                sl = slice(p * LANES, (p + 1) * LANES)
                d = pr[p]
                u = jnp.where(head_masks[0], rows_of(u_s, 2 * j), rows_of(u_s, 2 * j + 1))
                y = _dot_nt(d['r_t'], d['s']) + jnp.where(head_masks[0], rows_of(y_s, 2 * j), rows_of(y_s, 2 * j + 1))
                w_last = jnp.exp(d['cum'][L - 1:L, :])
                s_new = (d['s'] + _dot_tn(u, d['b_t']) + _dot_tn(d['v'], d['k_t'])) * w_last
                s_ref[b, p] = jnp.where(same_head, s_new, 0.0)

                mean = seg_sum(y) * (1.0 / R_DH)
                yc = y - mean
                var = seg_sum(yc * yc) * (1.0 / R_DH)
                yn = yc * lax.rsqrt(var + GN_EPS) * lnw_ref[:, sl] + lnb_ref[:, sl]
                bonus = seg_sum(d['r'] * d['k'] * rk_ref[:, sl]) * d['v']
                y_ref[b, :, sl] = _bf((yn + bonus) * g_all[:, sl])


def _rwkv(proj3, mu, wwa, w0, a0, g2, k_k, k_a, r_k, ln_w, ln_b, s0, sh0):
    B, T, _ = proj3.shape
    L = min(CHUNK, T)
    bb = 2
    NP = R_HEADS // 2
    vec = lambda n: pl.BlockSpec((1, n), lambda b, t: (0, 0))
    return pl.pallas_call(
        functools.partial(_rwkv_body, bb=bb, L=L),
        grid=(B // bb, T // L),
        in_specs=[pl.BlockSpec((bb, L, 4 * W_GRP), lambda b, t: (b, t, COL_R // (4 * W_GRP))),
                  vec(N_R),
                  pl.BlockSpec((LANES, 2 * W_GRP), lambda b, t: (0, 0)),
                  vec(W_GRP), vec(W_GRP),
                  pl.BlockSpec((R_LR_G, W_GRP), lambda b, t: (0, 0)),
                  vec(W_GRP), vec(W_GRP), vec(W_GRP), vec(W_GRP), vec(W_GRP),
                  pl.BlockSpec((bb, NP, LANES, LANES), lambda b, t: (b, 0, 0, 0)),
                  pl.BlockSpec((bb, 1, N_R), lambda b, t: (b, 0, 0))],
        out_specs=[pl.BlockSpec((bb, L, W_GRP), lambda b, t: (b, t, 0)),
                   pl.BlockSpec((bb, NP, LANES, LANES), lambda b, t: (b, 0, 0, 0)),
                   pl.BlockSpec((bb, 1, N_R), lambda b, t: (b, 0, 0))],
        out_shape=[jax.ShapeDtypeStruct((B, T, W_GRP), BF16),
                   jax.ShapeDtypeStruct((B, NP, LANES, LANES), F32),
                   jax.ShapeDtypeStruct((B, 1, N_R), F32)],
        scratch_shapes=[pltpu.VMEM((bb, 8 + L, N_R), F32)],
        compiler_params=_params(("arbitrary", "arbitrary")),
        name="rwkv7",
    )(proj3, mu, wwa, w0, a0, g2, k_k, k_a, r_k, ln_w, ln_b, s0, sh0)


def _s5_body(u_ref, win_ref, wout_ref, lam_ref, pw_ref, d_ref, wglu_ref, bglu_ref, hr0_ref, hi0_ref,
             y_ref, hr_ref, hi_ref, sr_ref, si_ref, *, Lb):
    @pl.when(pl.program_id(1) == 0)
    def _():
        hr_ref[...] = hr0_ref[...]
        hi_ref[...] = hi0_ref[...]

    NB = S5_N // W_GRP
    u = u_ref[0]
    for c in range(NB):
        bu = _dot(u[:, c * LANES:(c + 1) * LANES], win_ref[c])
        sr_ref[:, c * W_GRP:(c + 1) * W_GRP] = bu[:, 0:W_GRP]
        si_ref[:, c * W_GRP:(c + 1) * W_GRP] = bu[:, W_GRP:2 * W_GRP]

    row8 = _iota2((8, S5_N), 0)
    lam_r = [lam_ref[k:k + 1, :] for k in range(3)]
    lam_i = [lam_ref[3 + k:4 + k, :] for k in range(3)]
    pw_r = pw_ref[0:8, :]
    pw_i = pw_ref[8:16, :]

    def tile(n, carry):
        cr, ci = carry
        base = pl.multiple_of(n * 8, 8)
        xr = sr_ref[pl.ds(base, 8), :]
        xi = si_ref[pl.ds(base, 8), :]
        for lvl, sft in enumerate((1, 2, 4)):
            keep = row8 >= sft
            zr = jnp.where(keep, pltpu.roll(xr, sft, axis=0), 0.0)
            zi = jnp.where(keep, pltpu.roll(xi, sft, axis=0), 0.0)
            xr, xi = (xr + lam_r[lvl] * zr - lam_i[lvl] * zi,
                      xi + lam_r[lvl] * zi + lam_i[lvl] * zr)
        xr, xi = xr + pw_r * cr - pw_i * ci, xi + pw_r * ci + pw_i * cr
        sr_ref[pl.ds(base, 8), :] = xr
        si_ref[pl.ds(base, 8), :] = xi
        return xr[7:8, :], xi[7:8, :]

    cr, ci = lax.fori_loop(0, Lb // 8, tile, (hr_ref[0], hi_ref[0]))
    hr_ref[0] = cr
    hi_ref[0] = ci

    ys = []
    for c in range(NB):
        sl = slice(c * W_GRP, (c + 1) * W_GRP)
        ys.append(_dot(sr_ref[:, sl], wout_ref[c, 0]) + _dot(si_ref[:, sl], wout_ref[c, 1]))
    y = jnp.concatenate(ys, axis=-1) + d_ref[...] * u
    y = _gelu_tanh(y)
    y_ref[0] = _bf(y * _sigmoid(_dot(y, wglu_ref[...]) + bglu_ref[...]))


def _s5(proj3, win, wout, lam_pows, row_pows, d_vec, w_glu, b_glu, hr0, hi0):
    B, T, _ = proj3.shape
    Lb = min(512, T)
    st = pl.BlockSpec((1, 1, S5_N), lambda b, t: (b, 0, 0))
    full = lambda a: pl.BlockSpec(a.shape, lambda b, t: (0,) * a.ndim)
    return pl.pallas_call(
        functools.partial(_s5_body, Lb=Lb),
        grid=(B, T // Lb),
        in_specs=[pl.BlockSpec((1, Lb, W_GRP), lambda b, t: (b, t, COL_S // W_GRP)),
                  full(win), full(wout), full(lam_pows), full(row_pows), full(d_vec),
                  full(w_glu), full(b_glu), st, st],
        out_specs=[pl.BlockSpec((1, Lb, W_GRP), lambda b, t: (b, t, 0)), st, st],
        out_shape=[jax.ShapeDtypeStruct((B, T, W_GRP), BF16),
                   jax.ShapeDtypeStruct((B, 1, S5_N), F32),
                   jax.ShapeDtypeStruct((B, 1, S5_N), F32)],
        scratch_shapes=[pltpu.VMEM((Lb, S5_N), F32), pltpu.VMEM((Lb, S5_N), F32)],
        compiler_params=_params(("arbitrary", "arbitrary")),
        name="s5",
    )(proj3, win, wout, lam_pows, row_pows, d_vec, w_glu, b_glu, hr0, hi0)


def _ffn_body(x_ref, ym_ref, ys_ref, yr_ref, yg_ref, wo_ref, gpm_ref, gpf_ref, wg_ref, wu_ref, wd_ref, gpo_ref,
              o_ref, hf_ref, acc_ref):
    f = pl.program_id(1)

    @pl.when(f == 0)
    def _():
        mix = jnp.dot(ym_ref[...], wo_ref[0], preferred_element_type=F32)
        mix = mix + jnp.dot(ys_ref[...], wo_ref[1], preferred_element_type=F32)
        mix = mix + jnp.dot(yr_ref[...], wo_ref[2], preferred_element_type=F32)
        mix = mix + jnp.dot(yg_ref[...], wo_ref[3], preferred_element_type=F32)
        x1 = x_ref[...] + _rms(mix, gpm_ref[...])
        o_ref[...] = x1
        hf_ref[...] = _bf(_rms(x1, gpf_ref[...]))
        acc_ref[...] = jnp.zeros_like(acc_ref)

    hf = hf_ref[...]
    gate = jnp.dot(hf, wg_ref[...], preferred_element_type=F32)
    up = jnp.dot(hf, wu_ref[...], preferred_element_type=F32)
    acc_ref[...] += jnp.dot(_bf(_silu(gate) * up), wd_ref[...], preferred_element_type=F32)

    @pl.when(f == pl.num_programs(1) - 1)
    def _():
        o_ref[...] = o_ref[...] + _rms(acc_ref[...], gpo_ref[...])


def _out_ffn(x2, ym, ys, yr, yg, w_out, g_post_mix, g_pre_ffn, w_gate, w_up, w_down, g_post_ffn):
    n = x2.shape[0]
    tm = min(512, n)
    tf = 512
    row = lambda w: pl.BlockSpec((tm, w), lambda i, f: (i, 0))
    vec = pl.BlockSpec((1, D_MODEL), lambda i, f: (0, 0))
    return pl.pallas_call(
        _ffn_body,
        grid=(n // tm, D_FF // tf),
        in_specs=[row(D_MODEL), row(W_GRP), row(W_GRP), row(W_GRP), row(W_GRP),
                  pl.BlockSpec((4, W_GRP, D_MODEL), lambda i, f: (0, 0, 0), pipeline_mode=pl.Buffered(1)),
                  vec, vec,
                  pl.BlockSpec((D_MODEL, tf), lambda i, f: (0, f)),
                  pl.BlockSpec((D_MODEL, tf), lambda i, f: (0, f)),
                  pl.BlockSpec((tf, D_MODEL), lambda i, f: (f, 0)),
                  vec],
        out_specs=row(D_MODEL),
        out_shape=jax.ShapeDtypeStruct((n, D_MODEL), F32),
        scratch_shapes=[pltpu.VMEM((tm, D_MODEL), BF16), pltpu.VMEM((tm, D_MODEL), F32)],
        compiler_params=_params(("arbitrary", "arbitrary")),
        name="out_ffn",
    )(x2, ym, ys, yr, yg, w_out, g_post_mix, g_pre_ffn, w_gate, w_up, w_down, g_post_ffn)


def _lane_row(pairs):
    row = jnp.zeros((LANES,), F32)
    for off, vals in pairs:
        row = lax.dynamic_update_slice(row, vals.astype(F32), (off,))
    return row[None, :]


def _block_diag(blocks):
    n, r, c = blocks.shape
    eye = jnp.eye(n, dtype=blocks.dtype)
    return (eye[:, None, :, None] * blocks[:, :, None, :]).reshape(n * r, n * c)


def _complex_pow(re, im, n):
    pr, pi = re, im
    for _ in range(n - 1):
        pr, pi = pr * re - pi * im, pr * im + pi * re
    return pr, pi


def _prep_layer(p):
    f32 = lambda a: a.astype(F32)
    w_in = p['w_in']
    off_s = N_M
    off_r = N_M + N_S
    off_g = N_M + N_S + N_R
    gate_cols = jnp.concatenate([w_in[:, 4 * W_GRP:N_M], w_in[:, off_g + 4 * W_GRP:off_g + N_G]], axis=1)
    w_perm = jnp.concatenate([
        w_in[:, 0:4 * W_GRP],
        w_in[:, off_g:off_g + 4 * W_GRP],
        w_in[:, off_r:off_r + N_R],
        gate_cols,
        jnp.zeros((D_MODEL, COL_S - COL_GATE - 16), w_in.dtype),
        w_in[:, off_s:off_s + N_S]], axis=1)
    q = {'w_in': _bf(w_perm)}
    q['g_pre_mix'] = f32(p['g_pre_mix'])[None, :]

    gb = f32(p['mlstm_gate_bias'])
    q['m_gate_bias'] = _lane_row([(GATE_MI, gb[0]), (GATE_MF, gb[1])])
    q['m_norm_g'] = f32(p['mlstm_norm_g'])[None, :]

    lam_re, lam_im = f32(p['s5_lam_re']), f32(p['s5_lam_im'])
    dt = jnp.exp(f32(p['s5_log_dt']))[:, None]
    mag = jnp.exp(lam_re * dt)
    lb_re = mag * jnp.cos(lam_im * dt)
    lb_im = mag * jnp.sin(lam_im * dt)
    nr = lb_re - 1.0
    den = lam_re * lam_re + lam_im * lam_im
    f_re = (nr * lam_re + lb_im * lam_im) / den
    f_im = (lb_im * lam_re - nr * lam_im) / den
    B_re, B_im = f32(p['s5_B_re']), f32(p['s5_B_im'])
    Bb_re = f_re[..., None] * B_re - f_im[..., None] * B_im
    Bb_im = f_re[..., None] * B_im + f_im[..., None] * B_re
    nb, gpb = S5_N // W_GRP, S5_GROUPS // (S5_N // W_GRP)
    bd_in = lambda m: jnp.stack([_block_diag(jnp.swapaxes(m, 1, 2)[c * gpb:(c + 1) * gpb]) for c in range(nb)])
    q['s5_win'] = _bf(jnp.concatenate([bd_in(Bb_re), bd_in(Bb_im)], axis=-1))
    bd_out = lambda m: jnp.stack([_block_diag(jnp.swapaxes(m, 1, 2)[c * gpb:(c + 1) * gpb]) for c in range(nb)])
    q['s5_wout'] = _bf(jnp.stack([bd_out(f32(p['s5_C_re'])), -bd_out(f32(p['s5_C_im']))], axis=1))
    lr, li = lb_re.reshape(1, S5_N), lb_im.reshape(1, S5_N)
    pows = [_complex_pow(lr, li, n) for n in range(1, 9)]
    zero = jnp.zeros((2, S5_N), F32)
    q['s5_lam_pows'] = jnp.concatenate([pows[0][0], pows[1][0], pows[3][0], pows[0][1], pows[1][1], pows[3][1], zero], 0)
    q['s5_row_pows'] = jnp.concatenate([pw[0] for pw in pows] + [pw[1] for pw in pows], 0)
    q['s5_D'] = f32(p['s5_D'])[None, :]
    q['s5_w_glu'] = _bf(p['s5_w_glu'])
    q['s5_b_glu'] = f32(p['s5_b_glu'])[None, :]

    q['r_mu'] = f32(p['rwkv_mu'])[None, :]
    zw = jnp.zeros((R_LR_W, W_GRP), F32)
    q['r_wwa'] = _bf(jnp.concatenate([jnp.concatenate([f32(p['rwkv_w2']), zw], 1),
                                      jnp.concatenate([zw, f32(p['rwkv_a2'])], 1)], 0))
    q['r_w0'] = f32(p['rwkv_w0'])[None, :]
    q['r_a0'] = f32(p['rwkv_a0'])[None, :]
    q['r_g2'] = _bf(p['rwkv_g2'])
    q['r_k_k'] = f32(p['rwkv_k_k'])[None, :]
    q['r_k_a'] = f32(p['rwkv_k_a'])[None, :]
    q['r_r_k'] = f32(p['rwkv_r_k']).reshape(1, W_GRP)
    q['r_ln_w'] = f32(p['rwkv_ln_w'])[None, :]
    q['r_ln_b'] = f32(p['rwkv_ln_b'])[None, :]

    q['g_alog'] = _lane_row([(GATE_GA, f32(p['gdn_A_log']))])
    q['g_dtb'] = _lane_row([(GATE_GA, f32(p['gdn_dt_bias']))])
    q['g_conv_w'] = f32(p['gdn_conv_w'])
    q['g_norm_g'] = f32(p['gdn_norm_g'])[None, :]

    q['w_out'] = _bf(p['w_out']).reshape(4, W_GRP, D_MODEL)
    q['g_post_mix'] = f32(p['g_post_mix'])[None, :]
    q['g_pre_ffn'] = f32(p['g_pre_ffn'])[None, :]
    q['w_gate'] = _bf(p['w_gate'])
    q['w_up'] = _bf(p['w_up'])
    q['w_down'] = _bf(p['w_down'])
    q['g_post_ffn'] = f32(p['g_post_ffn'])[None, :]
    return q


def _pack_rwkv_state(s):
    B = s.shape[0]
    s = s.reshape(B, R_HEADS // 2, 2, R_DH, R_DH)
    z = jnp.zeros_like(s[:, :, 0])
    top = jnp.concatenate([s[:, :, 0], z], axis=-1)
    bot = jnp.concatenate([z, s[:, :, 1]], axis=-1)
    return jnp.concatenate([top, bot], axis=-2)


def _unpack_rwkv_state(s):
    B = s.shape[0]
    a = s[:, :, :R_DH, :R_DH]
    d = s[:, :, R_DH:, R_DH:]
    return jnp.stack([a, d], axis=2).reshape(B, R_HEADS, R_DH, R_DH)


def _layer(x, q, state):
    mC0, mn0, mm0, s5r0, s5i0, rS0, rsh0, gS0, gcv0 = state
    B, T, _ = x.shape
    x2 = x.reshape(B * T, D_MODEL)
    proj3 = _proj(x2, q['g_pre_mix'], q['w_in']).reshape(B, T, N_PROJ)

    mm0p = jnp.pad(mm0, ((0, 0), (0, LANES - M_HEADS)))[:, None, :]
    y_m, mC, mn, mmp = _mlstm(proj3, q['m_gate_bias'], q['m_norm_g'], mC0, mn0, mm0p)
    mm = mmp[:, 0, :M_HEADS]

    y_s, s5r, s5i = _s5(proj3, q['s5_win'], q['s5_wout'], q['s5_lam_pows'], q['s5_row_pows'], q['s5_D'],
                        q['s5_w_glu'], q['s5_b_glu'],
                        s5r0.reshape(B, 1, S5_N), s5i0.reshape(B, 1, S5_N))
    s5r = s5r.reshape(B, S5_GROUPS, S5_P)
    s5i = s5i.reshape(B, S5_GROUPS, S5_P)

    y_r, rSp, rsh = _rwkv(proj3, q['r_mu'], q['r_wwa'], q['r_w0'], q['r_a0'], q['r_g2'], q['r_k_k'], q['r_k_a'],
                          q['r_r_k'], q['r_ln_w'], q['r_ln_b'], _pack_rwkv_state(rS0), rsh0[:, None, :])
    rS = _unpack_rwkv_state(rSp)
    rsh = rsh[:, 0, :]

    y_g, gS, gcv = _gdn(proj3, q['g_alog'], q['g_dtb'], q['g_conv_w'], q['g_norm_g'], gS0, gcv0)

    w = lambda a: a.reshape(B * T, W_GRP)
    out = _out_ffn(x2, w(y_m), w(y_s), w(y_r), w(y_g), q['w_out'], q['g_post_mix'], q['g_pre_ffn'],
                   q['w_gate'], q['w_up'], q['w_down'], q['g_post_ffn'])
    return out.reshape(B, T, D_MODEL), (mC, mn, mm, s5r, s5i, rS, rsh, gS, gcv)


def _zero_state(b):
    z = lambda *s: jnp.zeros(s, F32)
    return (z(b, M_HEADS, M_DH, M_DH), z(b, M_HEADS, M_DH), z(b, M_HEADS),
            z(b, S5_GROUPS, S5_P), z(b, S5_GROUPS, S5_P),
            z(b, R_HEADS, R_DH, R_DH), z(b, N_R),
            z(b, G_HEADS, G_DH, G_DH), z(b, CONV_W - 1, 3 * W_GRP))


_PARAM_NAMES = ('g_pre_mix', 'w_in', 'mlstm_gate_bias', 'mlstm_norm_g',
                's5_lam_re', 's5_lam_im', 's5_log_dt', 's5_B_re', 's5_B_im', 's5_C_re', 's5_C_im',
                's5_D', 's5_w_glu', 's5_b_glu',
                'rwkv_mu', 'rwkv_w0', 'rwkv_w2', 'rwkv_a0', 'rwkv_a2', 'rwkv_g2', 'rwkv_k_k', 'rwkv_k_a',
                'rwkv_r_k', 'rwkv_ln_w', 'rwkv_ln_b',
                'gdn_conv_w', 'gdn_A_log', 'gdn_dt_bias', 'gdn_norm_g',
                'w_out', 'g_post_mix', 'g_pre_ffn', 'w_gate', 'w_up', 'w_down', 'g_post_ffn')


def kernel(x_prompt, x_sample, state_mlstm_C, state_mlstm_n, state_mlstm_m, state_s5_re, state_s5_im, state_rwkv_S, state_rwkv_shift, state_gdn_S, state_gdn_conv, g_pre_mix, w_in, mlstm_gate_bias, mlstm_norm_g, s5_lam_re, s5_lam_im, s5_log_dt, s5_B_re, s5_B_im, s5_C_re, s5_C_im, s5_D, s5_w_glu, s5_b_glu, rwkv_mu, rwkv_w0, rwkv_w2, rwkv_a0, rwkv_a2, rwkv_g2, rwkv_k_k, rwkv_k_a, rwkv_r_k, rwkv_ln_w, rwkv_ln_b, gdn_conv_w, gdn_A_log, gdn_dt_bias, gdn_norm_g, w_out, g_post_mix, g_pre_ffn, w_gate, w_up, w_down, g_post_ffn):
    weights = (g_pre_mix, w_in, mlstm_gate_bias, mlstm_norm_g,
               s5_lam_re, s5_lam_im, s5_log_dt, s5_B_re, s5_B_im, s5_C_re, s5_C_im, s5_D, s5_w_glu, s5_b_glu,
               rwkv_mu, rwkv_w0, rwkv_w2, rwkv_a0, rwkv_a2, rwkv_g2, rwkv_k_k, rwkv_k_a, rwkv_r_k,
               rwkv_ln_w, rwkv_ln_b, gdn_conv_w, gdn_A_log, gdn_dt_bias, gdn_norm_g,
               w_out, g_post_mix, g_pre_ffn, w_gate, w_up, w_down, g_post_ffn)
    caches = (state_mlstm_C, state_mlstm_n, state_mlstm_m, state_s5_re, state_s5_im,
              state_rwkv_S, state_rwkv_shift, state_gdn_S, state_gdn_conv)
    yp, ys = x_prompt, x_sample
    outs_p, outs_s = [], []
    for l in range(DEPTH):
        q = _prep_layer({name: wt[l] for name, wt in zip(_PARAM_NAMES, weights)})
        yp, st_p = _layer(yp, q, _zero_state(x_prompt.shape[0]))
        ys, st_s = _layer(ys, q, tuple(c[l].astype(F32) for c in caches))
        outs_p.append(st_p)
        outs_s.append(st_s)
    stack = lambda outs: [jnp.stack(t) for t in zip(*outs)]
    return (yp, ys, *stack(outs_p), *stack(outs_s))
```

```python
import functools
import math

import jax
import jax.numpy as jnp
from jax import lax
from jax.experimental import pallas as pl
from jax.experimental.pallas import tpu as pltpu

F32 = jnp.float32
BF16 = jnp.bfloat16

D_MODEL = 2048
DEPTH = 2
EPS = 1e-6
GN_EPS = 64e-5
W_GRP = D_MODEL // 4
M_HEADS, M_DH = 4, 128
S5_CH, S5_GROUPS, S5_P = 16, 32, 64
S5_N = S5_GROUPS * S5_P
R_DH, R_HEADS = 64, 8
R_LR_W, R_LR_A, R_LR_G = 64, 64, 128
G_HEADS, G_DH = 4, 128
CONV_W = 4
D_FF = 5632
N_M = 4 * W_GRP + 2 * M_HEADS
N_S = W_GRP
N_R = 3 * W_GRP + R_LR_W + R_LR_A + R_LR_G
N_G = 4 * W_GRP + 2 * G_HEADS

COL_M = 0
COL_G = 2048
COL_R = 4096
COL_GATE = COL_R + N_R
COL_S = 6144
N_PROJ = 6656
GATE_MI, GATE_MF, GATE_GA, GATE_GB = 0, 4, 8, 12

CHUNK = 64
CHUNKS_PER_STEP = 1
LANES = 128
VMEM_LIMIT = 56 * 1024 * 1024


def _bf(x):
    return x.astype(BF16)


def _dot(a, b):
    return jnp.dot(_bf(a), _bf(b), preferred_element_type=F32)


def _dot_nt(a, b):
    return lax.dot_general(_bf(a), _bf(b), (((1,), (1,)), ((), ())), preferred_element_type=F32)


def _dot_tn(a, b):
    return lax.dot_general(_bf(a), _bf(b), (((0,), (0,)), ((), ())), preferred_element_type=F32)


def _split3(x):
    hi = _bf(x)
    r1 = x - hi.astype(F32)
    mid = _bf(r1)
    lo = _bf(r1 - mid.astype(F32))
    return hi, mid, lo


def _dot_exact_rhs(a_bf, x):
    hi, mid, lo = _split3(x)
    d = lambda v: jnp.dot(a_bf, v, preferred_element_type=F32)
    return d(hi) + d(mid) + d(lo)


def _dot_exact_lhs(x, b_bf):
    hi, mid, lo = _split3(x)
    d = lambda v: jnp.dot(v, b_bf, preferred_element_type=F32)
    return d(hi) + d(mid) + d(lo)


def _dot_nt_exact_rhs(a_bf, x):
    hi, mid, lo = _split3(x)
    d = lambda v: lax.dot_general(a_bf, v, (((1,), (1,)), ((), ())), preferred_element_type=F32)
    return d(hi) + d(mid) + d(lo)


def _sigmoid(x):
    return 1.0 / (1.0 + jnp.exp(-x))


def _softplus(x):
    return jnp.maximum(x, 0.0) + jnp.log(1.0 + jnp.exp(-jnp.abs(x)))


def _log_sigmoid(x):
    return -_softplus(-x)


def _silu(x):
    return x * _sigmoid(x)


def _gelu_tanh(x):
    return 0.5 * x * (1.0 + jnp.tanh(math.sqrt(2.0 / math.pi) * (x + 0.044715 * (x * x * x))))


def _rms(x, g):
    return x * lax.rsqrt(jnp.mean(x * x, -1, keepdims=True) + EPS) * g


def _iota2(shape, axis):
    return lax.broadcasted_iota(jnp.int32, shape, axis)


def _tri_masks(n):
    r, c = _iota2((n, n), 0), _iota2((n, n), 1)
    return r >= c, r > c


def _ones_where(mask):
    return jnp.where(mask, 1.0, 0.0).astype(BF16)


def _neumann_tail(n_mat, order):
    tm = n_mat
    p = n_mat
    k = 1
    while 2 * k < order:
        p = _dot(p, p)
        tm = tm + p + _dot(tm, p)
        k *= 2
        yield
    return tm


def _run_interleaved(gens):
    gens = list(gens)
    while gens:
        alive = []
        for g in gens:
            try:
                next(g)
                alive.append(g)
            except StopIteration:
                pass
        gens = alive


def _block_masks(n_blocks, L):
    n = n_blocks * L
    r, c = _iota2((n, n), 0), _iota2((n, n), 1)
    blk = lambda i: sum(jnp.where(i >= j * L, 1, 0) for j in range(1, n_blocks))
    same = blk(r) == blk(c)
    return same & (r >= c), same & (r > c)


def _col(x, j):
    return x[:, j:j + 1]


def _row(x, i):
    return x[i:i + 1, :]


def _params(sem):
    return pltpu.CompilerParams(dimension_semantics=sem, vmem_limit_bytes=VMEM_LIMIT)


def _proj_body(x_ref, g_ref, w_ref, o_ref, h_ref):
    @pl.when(pl.program_id(1) == 0)
    def _():
        h_ref[...] = _bf(_rms(x_ref[...], g_ref[...]))
    o_ref[...] = jnp.dot(h_ref[...], w_ref[...], preferred_element_type=F32)


def _proj(x2, g, w):
    n = x2.shape[0]
    tm = min(512, n)
    tn = N_PROJ // 4
    return pl.pallas_call(
        _proj_body,
        grid=(n // tm, N_PROJ // tn),
        in_specs=[pl.BlockSpec((tm, D_MODEL), lambda i, j: (i, 0)),
                  pl.BlockSpec((1, D_MODEL), lambda i, j: (0, 0)),
                  pl.BlockSpec((D_MODEL, tn), lambda i, j: (0, j))],
        out_specs=pl.BlockSpec((tm, tn), lambda i, j: (i, j)),
        out_shape=jax.ShapeDtypeStruct((n, N_PROJ), F32),
        scratch_shapes=[pltpu.VMEM((tm, D_MODEL), BF16)],
        compiler_params=_params(("arbitrary", "arbitrary")),
        name="in_proj",
    )(x2, g, w)


def _mlstm_body(p_ref, gt_ref, gb_ref, ng_ref, c0_ref, n0_ref, m0_ref,
                y_ref, c_ref, n_ref, m_ref, *, bb, L):
    @pl.when(pl.program_id(1) == 0)
    def _():
        c_ref[...] = c0_ref[...]
        n_ref[...] = n0_ref[...]
        m_ref[...] = m0_ref[...]

    incl, _ = _tri_masks(L)
    tri_lo = _ones_where(incl)
    tri_up = _ones_where(_iota2((L, L), 0) <= _iota2((L, L), 1))
    sel = _ones_where(_iota2((8, LANES), 0) == _iota2((8, LANES), 1))
    lane = _iota2((1, LANES), 1)
    scale = M_DH ** -0.5

    heads, m_updates = [], []
    for b in range(bb):
        gates = gt_ref[b] + gb_ref[...]
        lf_cols = _log_sigmoid(gates)
        b_cols = _dot_exact_rhs(tri_lo, lf_cols)
        rows = _dot_nt_exact_rhs(sel, jnp.where(lane < GATE_MF, gates, lf_cols))
        b_rows = _dot_exact_lhs(rows, tri_up)
        m_row = m_ref[b]
        m_news = []

        def head(h, b=b, gates=gates, b_cols=b_cols, rows=rows, b_rows=b_rows, m_row=m_row, m_news=m_news):
            sl = slice(h * M_DH, (h + 1) * M_DH)
            q = p_ref[b, :, sl] * scale
            k = p_ref[b, :, W_GRP + h * M_DH:W_GRP + (h + 1) * M_DH]
            v = p_ref[b, :, 2 * W_GRP + h * M_DH:2 * W_GRP + (h + 1) * M_DH]
            o = p_ref[b, :, 3 * W_GRP + h * M_DH:3 * W_GRP + (h + 1) * M_DH]
            li_c = _col(gates, GATE_MI + h)
            b_c = _col(b_cols, GATE_MF + h)
            li_r = _row(rows, GATE_MI + h)
            b_r = _row(b_rows, GATE_MF + h)
            m_prev = jnp.sum(jnp.where(lane == h, m_row, 0.0), axis=1, keepdims=True)
            c_st = c_ref[b, h]
            n_st = n_ref[b, h:h + 1, :]

            log_d = jnp.where(incl, b_c - b_r + li_r, -jnp.inf)
            inter = b_c + m_prev
            mt = jnp.maximum(inter, jnp.max(log_d, axis=-1, keepdims=True))
            yield
            s = _dot_nt(q, k) * jnp.exp(log_d - mt)
            w_c = jnp.exp(inter - mt)
            yield
            num = _dot(s, v) + _dot(q, c_st) * w_c
            den = jnp.sum(s, -1, keepdims=True) + w_c * jnp.sum(q * n_st, -1, keepdims=True)
            hh = num / jnp.maximum(jnp.abs(den), jnp.exp(-mt))
            yield

            b_last = b_c[L - 1:L, :]
            tail = b_last - b_c + li_c
            m_new = jnp.maximum(b_last + m_prev, jnp.max(tail, axis=0, keepdims=True))
            wk = jnp.exp(tail - m_new)
            sc = jnp.exp(b_last + m_prev - m_new)
            c_ref[b, h] = sc * c_st + _dot_tn(k, wk * v)
            n_ref[b, h:h + 1, :] = sc * n_st + jnp.sum(wk * k, axis=0, keepdims=True)
            m_news.append((h, m_new))
            yield

            mu = jnp.mean(hh, -1, keepdims=True)
            hc = hh - mu
            hn = hc * lax.rsqrt(jnp.mean(hc * hc, -1, keepdims=True) + EPS)
            y_ref[b, :, sl] = _bf(_sigmoid(o) * (hn * ng_ref[:, sl]))

        heads.extend(head(h) for h in range(M_HEADS))
        m_updates.append((b, m_row, m_news))
    _run_interleaved(heads)
    for b, m_row, m_news in m_updates:
        m_out = m_row
        for h, m_new in m_news:
            m_out = jnp.where(lane == h, m_new, m_out)
        m_ref[b] = m_out


def _mlstm(proj3, gate_bias, norm_g, c0, n0, m0):
    B, T, _ = proj3.shape
    L = min(CHUNK, T)
    bb = 2
    st = lambda b, t: (b, 0, 0)
    return pl.pallas_call(
        functools.partial(_mlstm_body, bb=bb, L=L),
        grid=(B // bb, T // L),
        in_specs=[pl.BlockSpec((bb, L, 4 * W_GRP), lambda b, t: (b, t, COL_M // (4 * W_GRP))),
                  pl.BlockSpec((bb, L, LANES), lambda b, t: (b, t, COL_GATE // LANES)),
                  pl.BlockSpec((1, LANES), lambda b, t: (0, 0)),
                  pl.BlockSpec((1, W_GRP), lambda b, t: (0, 0)),
                  pl.BlockSpec((bb, M_HEADS, M_DH, M_DH), lambda b, t: (b, 0, 0, 0)),
                  pl.BlockSpec((bb, M_HEADS, M_DH), st),
                  pl.BlockSpec((bb, 1, LANES), st)],
        out_specs=[pl.BlockSpec((bb, L, W_GRP), lambda b, t: (b, t, 0)),
                   pl.BlockSpec((bb, M_HEADS, M_DH, M_DH), lambda b, t: (b, 0, 0, 0)),
                   pl.BlockSpec((bb, M_HEADS, M_DH), st),
                   pl.BlockSpec((bb, 1, LANES), st)],
        out_shape=[jax.ShapeDtypeStruct((B, T, W_GRP), BF16),
                   jax.ShapeDtypeStruct((B, M_HEADS, M_DH, M_DH), F32),
                   jax.ShapeDtypeStruct((B, M_HEADS, M_DH), F32),
                   jax.ShapeDtypeStruct((B, 1, LANES), F32)],
        compiler_params=_params(("arbitrary", "arbitrary")),
        name="mlstm",
    )(proj3, proj3, gate_bias, norm_g, c0, n0, m0)


def _gdn_body(p_ref, gt_ref, alog_ref, dtb_ref, cw_ref, ng_ref, s0_ref, cv0_ref,
              y_ref, s_ref, cv_ref, xp_ref, *, bb, L, Lb):
    t = pl.program_id(1)
    W3 = 3 * W_GRP
    PAD = 8
    H = G_HEADS

    @pl.when(t == 0)
    def _():
        s_ref[...] = s0_ref[...]
        xp_ref[:, PAD - (CONV_W - 1):PAD, :] = cv0_ref[...]

    incl, _ = _tri_masks(L)
    tri_lo = _ones_where(incl)
    incl_bd, strict_bd = _block_masks(H, L)
    sel = _ones_where(_iota2((8, LANES), 0) == _iota2((8, LANES), 1))
    lane = _iota2((1, LANES), 1)
    neg_a = -jnp.exp(alog_ref[...])
    rows_of = lambda x, h: x[h * L:(h + 1) * L]
    stack = lambda xs: jnp.concatenate(xs, axis=0)

    def chunk(b, c0, acc):
        gates = gt_ref[b, c0:c0 + L, :]
        g_cols = neg_a * _softplus(gates + dtb_ref[...])
        beta_cols = _sigmoid(gates)
        gc_cols = _dot_exact_rhs(tri_lo, g_cols)

        qs, ks, vs = [], [], []
        for h in range(H):
            q = acc[:, h * G_DH:(h + 1) * G_DH]
            k = acc[:, W_GRP + h * G_DH:W_GRP + (h + 1) * G_DH]
            qs.append(q * lax.rsqrt(jnp.sum(q * q, -1, keepdims=True) + 1e-6) * (G_DH ** -0.5))
            ks.append(k * lax.rsqrt(jnp.sum(k * k, -1, keepdims=True) + 1e-6))
            vs.append(acc[:, 2 * W_GRP + h * G_DH:2 * W_GRP + (h + 1) * G_DH])
        q_s, k_s, v_s = stack(qs), stack(ks), stack(vs)
        g_c = stack([_col(gc_cols, GATE_GA + h) for h in range(H)])
        beta = stack([_col(beta_cols, GATE_GB + h) for h in range(H)])
        g_r = _dot_nt_exact_rhs(sel, jnp.where(lane == 0, g_c, 0.0))[0:1, :]
        s_st = [s_ref[b, h] for h in range(H)]

        diff = g_c - g_r
        eg = jnp.exp(g_c)
        a_mat = beta * _dot_nt(k_s, k_s) * jnp.exp(jnp.where(strict_bd, diff, -jnp.inf))
        yield
        tm = yield from _neumann_tail(-a_mat, L)
        rhs = jnp.concatenate([beta * v_s, (beta * eg) * k_s], axis=1)
        sol = rhs + _dot(tm, rhs)
        yield
        u = sol[:, 0:G_DH] - stack([_dot(rows_of(sol, h)[:, G_DH:2 * G_DH], s_st[h]) for h in range(H)])
        qk = _dot_nt(q_s, k_s) * jnp.exp(jnp.where(incl_bd, diff, -jnp.inf))
        yield
        o_s = eg * stack([_dot(qs[h], s_st[h]) for h in range(H)]) + _dot(qk, u)
        yield

        for h in range(H):
            sl = slice(h * G_DH, (h + 1) * G_DH)
            g_ch = rows_of(g_c, h)
            g_last = g_ch[L - 1:L, :]
            s_ref[b, h] = jnp.exp(g_last) * s_st[h] + _dot_tn(ks[h] * jnp.exp(g_last - g_ch), rows_of(u, h))
            o = rows_of(o_s, h)
            o = o * lax.rsqrt(jnp.mean(o * o, -1, keepdims=True) + EPS) * ng_ref[...]
            z = p_ref[b, c0:c0 + L, W3 + h * G_DH:W3 + (h + 1) * G_DH]
            y_ref[b, c0:c0 + L, sl] = _bf(o * _silu(z))

    def stream(b, acc):
        for c0 in range(0, Lb, L):
            yield from chunk(b, c0, acc[c0:c0 + L])

    streams = []
    for b in range(bb):
        xp_ref[b, PAD:PAD + Lb, :] = p_ref[b, :, 0:W3]
        acc = xp_ref[b, PAD - 3:PAD - 3 + Lb, :] * cw_ref[0:1, :]
        for j in range(1, CONV_W):
            acc = acc + xp_ref[b, PAD - 3 + j:PAD - 3 + j + Lb, :] * cw_ref[j:j + 1, :]
        tail3 = xp_ref[b, PAD + Lb - (CONV_W - 1):PAD + Lb, :]
        xp_ref[b, PAD - (CONV_W - 1):PAD, :] = tail3
        cv_ref[b] = tail3
        streams.append(stream(b, _silu(acc)))
    _run_interleaved(streams)


def _gdn(proj3, alog_row, dtb_row, conv_w, norm_g, s0, cv0):
    B, T, _ = proj3.shape
    L = min(CHUNK, T)
    Lb = min(CHUNKS_PER_STEP * L, T)
    bb = 2
    return pl.pallas_call(
        functools.partial(_gdn_body, bb=bb, L=L, Lb=Lb),
        grid=(B // bb, T // Lb),
        in_specs=[pl.BlockSpec((bb, Lb, 4 * W_GRP), lambda b, t: (b, t, COL_G // (4 * W_GRP))),
                  pl.BlockSpec((bb, Lb, LANES), lambda b, t: (b, t, COL_GATE // LANES)),
                  pl.BlockSpec((1, LANES), lambda b, t: (0, 0)),
                  pl.BlockSpec((1, LANES), lambda b, t: (0, 0)),
                  pl.BlockSpec((CONV_W, 3 * W_GRP), lambda b, t: (0, 0)),
                  pl.BlockSpec((1, G_DH), lambda b, t: (0, 0)),
                  pl.BlockSpec((bb, G_HEADS, G_DH, G_DH), lambda b, t: (b, 0, 0, 0)),
                  pl.BlockSpec((bb, CONV_W - 1, 3 * W_GRP), lambda b, t: (b, 0, 0))],
        out_specs=[pl.BlockSpec((bb, Lb, W_GRP), lambda b, t: (b, t, 0)),
                   pl.BlockSpec((bb, G_HEADS, G_DH, G_DH), lambda b, t: (b, 0, 0, 0)),
                   pl.BlockSpec((bb, CONV_W - 1, 3 * W_GRP), lambda b, t: (b, 0, 0))],
        out_shape=[jax.ShapeDtypeStruct((B, T, W_GRP), BF16),
                   jax.ShapeDtypeStruct((B, G_HEADS, G_DH, G_DH), F32),
                   jax.ShapeDtypeStruct((B, CONV_W - 1, 3 * W_GRP), F32)],
        scratch_shapes=[pltpu.VMEM((bb, 8 + Lb, 3 * W_GRP), F32)],
        compiler_params=_params(("arbitrary", "arbitrary")),
        name="gdn",
    )(proj3, proj3, alog_row, dtb_row, conv_w, norm_g, s0, cv0)


def _rwkv_body(p_ref, mu_ref, wwa_ref, w0_ref, a0_ref, g2_ref, kk_ref, ka_ref, rk_ref,
               lnw_ref, lnb_ref, s0_ref, sh0_ref,
               y_ref, s_ref, sh_ref, xs_ref, *, bb, L, Lb):
    t = pl.program_id(1)
    PAD = 8
    NQ = 4

    @pl.when(t == 0)
    def _():
        s_ref[...] = s0_ref[...]
        xs_ref[:, PAD - 1:PAD, :] = sh0_ref[...]

    incl, _ = _tri_masks(L)
    tri_lo = _ones_where(incl)
    incl_bd, strict_bd = _block_masks(NQ, L)
    lane = _iota2((1, LANES), 1)
    head_masks = (lane < R_DH, lane >= R_DH)
    r_i, c_i = _iota2((LANES, LANES), 0), _iota2((LANES, LANES), 1)
    same_head = (r_i < R_DH) == (c_i < R_DH)
    seg = _ones_where(same_head)
    seg_sum = lambda x: _dot_exact_lhs(x, seg)
    rows_of = lambda x, i: x[i * L:(i + 1) * L]
    stack = lambda xs: jnp.concatenate(xs, axis=0)

    groups = []
    for b in range(bb):
        xs_ref[b, PAD:PAD + Lb, :] = p_ref[b, :, 0:N_R]
        cur = p_ref[b, :, 0:N_R]
        prev = xs_ref[b, PAD - 1:PAD - 1 + Lb, :]
        last = xs_ref[b, PAD + Lb - 1:PAD + Lb, :]
        xs_ref[b, PAD - 1:PAD, :] = last
        sh_ref[b] = last

        xm = cur + (prev - cur) * mu_ref[...]
        r_all = xm[:, 0:W_GRP]
        k_all = xm[:, W_GRP:2 * W_GRP]
        v_all = xm[:, 2 * W_GRP:3 * W_GRP]
        wa_code = xm[:, 3 * W_GRP:3 * W_GRP + LANES]
        g_code = xm[:, 3 * W_GRP + LANES:3 * W_GRP + 2 * LANES]
        wa_in = jnp.where(lane < R_LR_W, jnp.tanh(wa_code), wa_code)
        wa = _dot(wa_in, wwa_ref[...])
        w_all = -_softplus(-(w0_ref[...] + wa[:, 0:W_GRP])) - 0.5
        ld_all = -jnp.exp(w_all)
        a_all = _sigmoid(a0_ref[...] + wa[:, W_GRP:2 * W_GRP])
        g_all = _dot(_sigmoid(g_code), g2_ref[...])
        k2_all = k_all * (1.0 + (a_all - 1.0) * ka_ref[...])
        kkr_all = k_all * kk_ref[...]

        def quad(qd, c0, b=b, r_all=r_all, k2_all=k2_all, v_all=v_all, ld_all=ld_all, a_all=a_all,
                 kkr_all=kkr_all, g_all=g_all):
            rs = slice(c0, c0 + L)
            pairs = (2 * qd, 2 * qd + 1)
            pr = {}
            for p in pairs:
                sl = slice(p * LANES, (p + 1) * LANES)
                r, k, v, ld, a, kkr = (r_all[rs, sl], k2_all[rs, sl], v_all[rs, sl], ld_all[rs, sl], a_all[rs, sl],
                                       kkr_all[rs, sl])
                kk = kkr * lax.rsqrt(seg_sum(kkr * kkr) + 1e-6)
                cum = _dot_exact_rhs(tri_lo, ld)
                e_neg = jnp.exp(-cum)
                pr[p] = dict(r=r, k=k, v=v, cum=cum, a_t=jnp.exp(cum - ld) * kk, b_t=-(kk * a) * e_neg,
                             k_t=k * e_neg, r_t=r * jnp.exp(cum), s=s_ref[b, p])
            slots = [(p, hm) for p in pairs for hm in head_masks]
            a_s = stack([jnp.where(hm, pr[p]['a_t'], 0.0) for p, hm in slots])
            r_s = stack([jnp.where(hm, pr[p]['r_t'], 0.0) for p, hm in slots])
            b_s = stack([pr[p]['b_t'] for p, _ in slots])
            k_s = stack([pr[p]['k_t'] for p, _ in slots])
            v_s = stack([pr[p]['v'] for p, _ in slots])
            a_ab = jnp.where(strict_bd, _dot_nt(a_s, b_s), 0.0)
            a_ak = jnp.where(strict_bd, _dot_nt(a_s, k_s), 0.0)
            a_rb = jnp.where(incl_bd, _dot_nt(r_s, b_s), 0.0)
            a_rk = jnp.where(incl_bd, _dot_nt(r_s, k_s), 0.0)
            yield
            tm = yield from _neumann_tail(a_ab, L)
            rhs = stack([_dot_nt(rows_of(a_s, i), pr[p]['s']) for i, (p, _) in enumerate(slots)]) + _dot(a_ak, v_s)
            yield
            u_s = rhs + _dot(tm, rhs)
            yield
            y_s = _dot(a_rb, u_s) + _dot(a_rk, v_s)
            yield

            for j, p in enumerate(pairs):
                sl = slice(p * LANES, (p + 1) * LANES)
                d = pr[p]
                u = jnp.where(head_masks[0], rows_of(u_s, 2 * j), rows_of(u_s, 2 * j + 1))
                y = _dot_nt(d['r_t'], d['s']) + jnp.where(head_masks[0], rows_of(y_s, 2 * j), rows_of(y_s, 2 * j + 1))
                w_last = jnp.exp(d['cum'][L - 1:L, :])
                s_new = (d['s'] + _dot_tn(u, d['b_t']) + _dot_tn(d['v'], d['k_t'])) * w_last
                s_ref[b, p] = jnp.where(same_head, s_new, 0.0)

                mean = seg_sum(y) * (1.0 / R_DH)
                yc = y - mean
                var = seg_sum(yc * yc) * (1.0 / R_DH)
                yn = yc * lax.rsqrt(var + GN_EPS) * lnw_ref[:, sl] + lnb_ref[:, sl]
                bonus = seg_sum(d['r'] * d['k'] * rk_ref[:, sl]) * d['v']
                y_ref[b, rs, sl] = _bf((yn + bonus) * g_all[rs, sl])
                yield

        def group(qd, quad=quad):
            for c0 in range(0, Lb, L):
                yield from quad(qd, c0)

        groups.extend(group(qd) for qd in range(R_HEADS // NQ))
    _run_interleaved(groups)


def _rwkv(proj3, mu, wwa, w0, a0, g2, k_k, k_a, r_k, ln_w, ln_b, s0, sh0):
    B, T, _ = proj3.shape
    L = min(CHUNK, T)
    Lb = min(CHUNKS_PER_STEP * L, T)
    bb = 2
    NP = R_HEADS // 2
    vec = lambda n: pl.BlockSpec((1, n), lambda b, t: (0, 0))
    return pl.pallas_call(
        functools.partial(_rwkv_body, bb=bb, L=L, Lb=Lb),
        grid=(B // bb, T // Lb),
        in_specs=[pl.BlockSpec((bb, Lb, 4 * W_GRP), lambda b, t: (b, t, COL_R // (4 * W_GRP))),
                  vec(N_R),
                  pl.BlockSpec((LANES, 2 * W_GRP), lambda b, t: (0, 0)),
                  vec(W_GRP), vec(W_GRP),
                  pl.BlockSpec((R_LR_G, W_GRP), lambda b, t: (0, 0)),
                  vec(W_GRP), vec(W_GRP), vec(W_GRP), vec(W_GRP), vec(W_GRP),
                  pl.BlockSpec((bb, NP, LANES, LANES), lambda b, t: (b, 0, 0, 0)),
                  pl.BlockSpec((bb, 1, N_R), lambda b, t: (b, 0, 0))],
        out_specs=[pl.BlockSpec((bb, Lb, W_GRP), lambda b, t: (b, t, 0)),
                   pl.BlockSpec((bb, NP, LANES, LANES), lambda b, t: (b, 0, 0, 0)),
                   pl.BlockSpec((bb, 1, N_R), lambda b, t: (b, 0, 0))],
        out_shape=[jax.ShapeDtypeStruct((B, T, W_GRP), BF16),
                   jax.ShapeDtypeStruct((B, NP, LANES, LANES), F32),
                   jax.ShapeDtypeStruct((B, 1, N_R), F32)],
        scratch_shapes=[pltpu.VMEM((bb, 8 + Lb, N_R), F32)],
        compiler_params=_params(("arbitrary", "arbitrary")),
        name="rwkv7",
    )(proj3, mu, wwa, w0, a0, g2, k_k, k_a, r_k, ln_w, ln_b, s0, sh0)


def _s5_body(u_ref, win_ref, wout_ref, lam_ref, pw_ref, d_ref, wglu_ref, bglu_ref, hr0_ref, hi0_ref,
             y_ref, hr_ref, hi_ref, sr_ref, si_ref, *, Lb):
    @pl.when(pl.program_id(1) == 0)
    def _():
        hr_ref[...] = hr0_ref[...]
        hi_ref[...] = hi0_ref[...]

    NB = S5_N // W_GRP
    u = u_ref[0]
    for c in range(NB):
        bu = _dot(u[:, c * LANES:(c + 1) * LANES], win_ref[c])
        sr_ref[:, c * W_GRP:(c + 1) * W_GRP] = bu[:, 0:W_GRP]
        si_ref[:, c * W_GRP:(c + 1) * W_GRP] = bu[:, W_GRP:2 * W_GRP]

    row8 = _iota2((8, S5_N), 0)
    lam_r = [lam_ref[k:k + 1, :] for k in range(3)]
    lam_i = [lam_ref[3 + k:4 + k, :] for k in range(3)]
    pw_r = pw_ref[0:8, :]
    pw_i = pw_ref[8:16, :]

    def tile(n, carry):
        cr, ci = carry
        base = pl.multiple_of(n * 8, 8)
        xr = sr_ref[pl.ds(base, 8), :]
        xi = si_ref[pl.ds(base, 8), :]
        for lvl, sft in enumerate((1, 2, 4)):
            keep = row8 >= sft
            zr = jnp.where(keep, pltpu.roll(xr, sft, axis=0), 0.0)
            zi = jnp.where(keep, pltpu.roll(xi, sft, axis=0), 0.0)
            xr, xi = (xr + lam_r[lvl] * zr - lam_i[lvl] * zi,
                      xi + lam_r[lvl] * zi + lam_i[lvl] * zr)
        xr, xi = xr + pw_r * cr - pw_i * ci, xi + pw_r * ci + pw_i * cr
        sr_ref[pl.ds(base, 8), :] = xr
        si_ref[pl.ds(base, 8), :] = xi
        return xr[7:8, :], xi[7:8, :]

    cr, ci = lax.fori_loop(0, Lb // 8, tile, (hr_ref[0], hi_ref[0]))
    hr_ref[0] = cr
    hi_ref[0] = ci

    ys = []
    for c in range(NB):
        sl = slice(c * W_GRP, (c + 1) * W_GRP)
        ys.append(_dot(sr_ref[:, sl], wout_ref[c, 0]) + _dot(si_ref[:, sl], wout_ref[c, 1]))
    y = jnp.concatenate(ys, axis=-1) + d_ref[...] * u
    y = _gelu_tanh(y)
    y_ref[0] = _bf(y * _sigmoid(_dot(y, wglu_ref[...]) + bglu_ref[...]))


def _s5(proj3, win, wout, lam_pows, row_pows, d_vec, w_glu, b_glu, hr0, hi0):
    B, T, _ = proj3.shape
    Lb = min(512, T)
    st = pl.BlockSpec((1, 1, S5_N), lambda b, t: (b, 0, 0))
    full = lambda a: pl.BlockSpec(a.shape, lambda b, t: (0,) * a.ndim)
    return pl.pallas_call(
        functools.partial(_s5_body, Lb=Lb),
        grid=(B, T // Lb),
        in_specs=[pl.BlockSpec((1, Lb, W_GRP), lambda b, t: (b, t, COL_S // W_GRP)),
                  full(win), full(wout), full(lam_pows), full(row_pows), full(d_vec),
                  full(w_glu), full(b_glu), st, st],
        out_specs=[pl.BlockSpec((1, Lb, W_GRP), lambda b, t: (b, t, 0)), st, st],
        out_shape=[jax.ShapeDtypeStruct((B, T, W_GRP), BF16),
                   jax.ShapeDtypeStruct((B, 1, S5_N), F32),
                   jax.ShapeDtypeStruct((B, 1, S5_N), F32)],
        scratch_shapes=[pltpu.VMEM((Lb, S5_N), F32), pltpu.VMEM((Lb, S5_N), F32)],
        compiler_params=_params(("arbitrary", "arbitrary")),
        name="s5",
    )(proj3, win, wout, lam_pows, row_pows, d_vec, w_glu, b_glu, hr0, hi0)


def _ffn_body(x_ref, ym_ref, ys_ref, yr_ref, yg_ref, wo_ref, gpm_ref, gpf_ref, wg_ref, wu_ref, wd_ref, gpo_ref,
              o_ref, hf_ref, acc_ref):
    f = pl.program_id(1)

    @pl.when(f == 0)
    def _():
        mix = jnp.dot(ym_ref[...], wo_ref[0], preferred_element_type=F32)
        mix = mix + jnp.dot(ys_ref[...], wo_ref[1], preferred_element_type=F32)
        mix = mix + jnp.dot(yr_ref[...], wo_ref[2], preferred_element_type=F32)
        mix = mix + jnp.dot(yg_ref[...], wo_ref[3], preferred_element_type=F32)
        x1 = x_ref[...] + _rms(mix, gpm_ref[...])
        o_ref[...] = x1
        hf_ref[...] = _bf(_rms(x1, gpf_ref[...]))
        acc_ref[...] = jnp.zeros_like(acc_ref)

    hf = hf_ref[...]
    gate = jnp.dot(hf, wg_ref[...], preferred_element_type=F32)
    up = jnp.dot(hf, wu_ref[...], preferred_element_type=F32)
    acc_ref[...] += jnp.dot(_bf(_silu(gate) * up), wd_ref[...], preferred_element_type=F32)

    @pl.when(f == pl.num_programs(1) - 1)
    def _():
        o_ref[...] = o_ref[...] + _rms(acc_ref[...], gpo_ref[...])


def _out_ffn(x2, ym, ys, yr, yg, w_out, g_post_mix, g_pre_ffn, w_gate, w_up, w_down, g_post_ffn):
    n = x2.shape[0]
    tm = min(512, n)
    tf = 512
    row = lambda w: pl.BlockSpec((tm, w), lambda i, f: (i, 0))
    vec = pl.BlockSpec((1, D_MODEL), lambda i, f: (0, 0))
    return pl.pallas_call(
        _ffn_body,
        grid=(n // tm, D_FF // tf),
        in_specs=[row(D_MODEL), row(W_GRP), row(W_GRP), row(W_GRP), row(W_GRP),
                  pl.BlockSpec((4, W_GRP, D_MODEL), lambda i, f: (0, 0, 0), pipeline_mode=pl.Buffered(1)),
                  vec, vec,
                  pl.BlockSpec((D_MODEL, tf), lambda i, f: (0, f)),
                  pl.BlockSpec((D_MODEL, tf), lambda i, f: (0, f)),
                  pl.BlockSpec((tf, D_MODEL), lambda i, f: (f, 0)),
                  vec],
        out_specs=row(D_MODEL),
        out_shape=jax.ShapeDtypeStruct((n, D_MODEL), F32),
        scratch_shapes=[pltpu.VMEM((tm, D_MODEL), BF16), pltpu.VMEM((tm, D_MODEL), F32)],
        compiler_params=_params(("arbitrary", "arbitrary")),
        name="out_ffn",
    )(x2, ym, ys, yr, yg, w_out, g_post_mix, g_pre_ffn, w_gate, w_up, w_down, g_post_ffn)


def _lane_row(pairs):
    row = jnp.zeros((LANES,), F32)
    for off, vals in pairs:
        row = lax.dynamic_update_slice(row, vals.astype(F32), (off,))
    return row[None, :]


def _block_diag(blocks):
    n, r, c = blocks.shape
    eye = jnp.eye(n, dtype=blocks.dtype)
    return (eye[:, None, :, None] * blocks[:, :, None, :]).reshape(n * r, n * c)


def _complex_pow(re, im, n):
    pr, pi = re, im
    for _ in range(n - 1):
        pr, pi = pr * re - pi * im, pr * im + pi * re
    return pr, pi


def _prep_layer(p):
    f32 = lambda a: a.astype(F32)
    w_in = p['w_in']
    off_s = N_M
    off_r = N_M + N_S
    off_g = N_M + N_S + N_R
    gate_cols = jnp.concatenate([w_in[:, 4 * W_GRP:N_M], w_in[:, off_g + 4 * W_GRP:off_g + N_G]], axis=1)
    w_perm = jnp.concatenate([
        w_in[:, 0:4 * W_GRP],
        w_in[:, off_g:off_g + 4 * W_GRP],
        w_in[:, off_r:off_r + N_R],
        gate_cols,
        jnp.zeros((D_MODEL, COL_S - COL_GATE - 16), w_in.dtype),
        w_in[:, off_s:off_s + N_S]], axis=1)
    q = {'w_in': _bf(w_perm)}
    q['g_pre_mix'] = f32(p['g_pre_mix'])[None, :]

    gb = f32(p['mlstm_gate_bias'])
    q['m_gate_bias'] = _lane_row([(GATE_MI, gb[0]), (GATE_MF, gb[1])])
    q['m_norm_g'] = f32(p['mlstm_norm_g'])[None, :]

    lam_re, lam_im = f32(p['s5_lam_re']), f32(p['s5_lam_im'])
    dt = jnp.exp(f32(p['s5_log_dt']))[:, None]
    mag = jnp.exp(lam_re * dt)
    lb_re = mag * jnp.cos(lam_im * dt)
    lb_im = mag * jnp.sin(lam_im * dt)
    nr = lb_re - 1.0
    den = lam_re * lam_re + lam_im * lam_im
    f_re = (nr * lam_re + lb_im * lam_im) / den
    f_im = (lb_im * lam_re - nr * lam_im) / den
    B_re, B_im = f32(p['s5_B_re']), f32(p['s5_B_im'])
    Bb_re = f_re[..., None] * B_re - f_im[..., None] * B_im
    Bb_im = f_re[..., None] * B_im + f_im[..., None] * B_re
    nb, gpb = S5_N // W_GRP, S5_GROUPS // (S5_N // W_GRP)
    bd_in = lambda m: jnp.stack([_block_diag(jnp.swapaxes(m, 1, 2)[c * gpb:(c + 1) * gpb]) for c in range(nb)])
    q['s5_win'] = _bf(jnp.concatenate([bd_in(Bb_re), bd_in(Bb_im)], axis=-1))
    bd_out = lambda m: jnp.stack([_block_diag(jnp.swapaxes(m, 1, 2)[c * gpb:(c + 1) * gpb]) for c in range(nb)])
    q['s5_wout'] = _bf(jnp.stack([bd_out(f32(p['s5_C_re'])), -bd_out(f32(p['s5_C_im']))], axis=1))
    lr, li = lb_re.reshape(1, S5_N), lb_im.reshape(1, S5_N)
    pows = [_complex_pow(lr, li, n) for n in range(1, 9)]
    zero = jnp.zeros((2, S5_N), F32)
    q['s5_lam_pows'] = jnp.concatenate([pows[0][0], pows[1][0], pows[3][0], pows[0][1], pows[1][1], pows[3][1], zero], 0)
    q['s5_row_pows'] = jnp.concatenate([pw[0] for pw in pows] + [pw[1] for pw in pows], 0)
    q['s5_D'] = f32(p['s5_D'])[None, :]
    q['s5_w_glu'] = _bf(p['s5_w_glu'])
    q['s5_b_glu'] = f32(p['s5_b_glu'])[None, :]

    q['r_mu'] = f32(p['rwkv_mu'])[None, :]
    zw = jnp.zeros((R_LR_W, W_GRP), F32)
    q['r_wwa'] = _bf(jnp.concatenate([jnp.concatenate([f32(p['rwkv_w2']), zw], 1),
                                      jnp.concatenate([zw, f32(p['rwkv_a2'])], 1)], 0))
    q['r_w0'] = f32(p['rwkv_w0'])[None, :]
    q['r_a0'] = f32(p['rwkv_a0'])[None, :]
    q['r_g2'] = _bf(p['rwkv_g2'])
    q['r_k_k'] = f32(p['rwkv_k_k'])[None, :]
    q['r_k_a'] = f32(p['rwkv_k_a'])[None, :]
    q['r_r_k'] = f32(p['rwkv_r_k']).reshape(1, W_GRP)
    q['r_ln_w'] = f32(p['rwkv_ln_w'])[None, :]
    q['r_ln_b'] = f32(p['rwkv_ln_b'])[None, :]

    q['g_alog'] = _lane_row([(GATE_GA, f32(p['gdn_A_log']))])
    q['g_dtb'] = _lane_row([(GATE_GA, f32(p['gdn_dt_bias']))])
    q['g_conv_w'] = f32(p['gdn_conv_w'])
    q['g_norm_g'] = f32(p['gdn_norm_g'])[None, :]

    q['w_out'] = _bf(p['w_out']).reshape(4, W_GRP, D_MODEL)
    q['g_post_mix'] = f32(p['g_post_mix'])[None, :]
    q['g_pre_ffn'] = f32(p['g_pre_ffn'])[None, :]
    q['w_gate'] = _bf(p['w_gate'])
    q['w_up'] = _bf(p['w_up'])
    q['w_down'] = _bf(p['w_down'])
    q['g_post_ffn'] = f32(p['g_post_ffn'])[None, :]
    return q


def _pack_rwkv_state(s):
    B = s.shape[0]
    s = s.reshape(B, R_HEADS // 2, 2, R_DH, R_DH)
    z = jnp.zeros_like(s[:, :, 0])
    top = jnp.concatenate([s[:, :, 0], z], axis=-1)
    bot = jnp.concatenate([z, s[:, :, 1]], axis=-1)
    return jnp.concatenate([top, bot], axis=-2)


def _unpack_rwkv_state(s):
    B = s.shape[0]
    a = s[:, :, :R_DH, :R_DH]
    d = s[:, :, R_DH:, R_DH:]
    return jnp.stack([a, d], axis=2).reshape(B, R_HEADS, R_DH, R_DH)


def _layer(x, q, state):
    mC0, mn0, mm0, s5r0, s5i0, rS0, rsh0, gS0, gcv0 = state
    B, T, _ = x.shape
    x2 = x.reshape(B * T, D_MODEL)
    proj3 = _proj(x2, q['g_pre_mix'], q['w_in']).reshape(B, T, N_PROJ)

    mm0p = jnp.pad(mm0, ((0, 0), (0, LANES - M_HEADS)))[:, None, :]
    y_m, mC, mn, mmp = _mlstm(proj3, q['m_gate_bias'], q['m_norm_g'], mC0, mn0, mm0p)
    mm = mmp[:, 0, :M_HEADS]

    y_s, s5r, s5i = _s5(proj3, q['s5_win'], q['s5_wout'], q['s5_lam_pows'], q['s5_row_pows'], q['s5_D'],
                        q['s5_w_glu'], q['s5_b_glu'],
                        s5r0.reshape(B, 1, S5_N), s5i0.reshape(B, 1, S5_N))
    s5r = s5r.reshape(B, S5_GROUPS, S5_P)
    s5i = s5i.reshape(B, S5_GROUPS, S5_P)

    y_r, rSp, rsh = _rwkv(proj3, q['r_mu'], q['r_wwa'], q['r_w0'], q['r_a0'], q['r_g2'], q['r_k_k'], q['r_k_a'],
                          q['r_r_k'], q['r_ln_w'], q['r_ln_b'], _pack_rwkv_state(rS0), rsh0[:, None, :])
    rS = _unpack_rwkv_state(rSp)
    rsh = rsh[:, 0, :]

    y_g, gS, gcv = _gdn(proj3, q['g_alog'], q['g_dtb'], q['g_conv_w'], q['g_norm_g'], gS0, gcv0)

    w = lambda a: a.reshape(B * T, W_GRP)
    out = _out_ffn(x2, w(y_m), w(y_s), w(y_r), w(y_g), q['w_out'], q['g_post_mix'], q['g_pre_ffn'],
                   q['w_gate'], q['w_up'], q['w_down'], q['g_post_ffn'])
    return out.reshape(B, T, D_MODEL), (mC, mn, mm, s5r, s5i, rS, rsh, gS, gcv)


def _zero_state(b):
    z = lambda *s: jnp.zeros(s, F32)
    return (z(b, M_HEADS, M_DH, M_DH), z(b, M_HEADS, M_DH), z(b, M_HEADS),
            z(b, S5_GROUPS, S5_P), z(b, S5_GROUPS, S5_P),
            z(b, R_HEADS, R_DH, R_DH), z(b, N_R),
            z(b, G_HEADS, G_DH, G_DH), z(b, CONV_W - 1, 3 * W_GRP))


_PARAM_NAMES = ('g_pre_mix', 'w_in', 'mlstm_gate_bias', 'mlstm_norm_g',
                's5_lam_re', 's5_lam_im', 's5_log_dt', 's5_B_re', 's5_B_im', 's5_C_re', 's5_C_im',
                's5_D', 's5_w_glu', 's5_b_glu',
                'rwkv_mu', 'rwkv_w0', 'rwkv_w2', 'rwkv_a0', 'rwkv_a2', 'rwkv_g2', 'rwkv_k_k', 'rwkv_k_a',
                'rwkv_r_k', 'rwkv_ln_w', 'rwkv_ln_b',
                'gdn_conv_w', 'gdn_A_log', 'gdn_dt_bias', 'gdn_norm_g',
                'w_out', 'g_post_mix', 'g_pre_ffn', 'w_gate', 'w_up', 'w_down', 'g_post_ffn')


def kernel(x_prompt, x_sample, state_mlstm_C, state_mlstm_n, state_mlstm_m, state_s5_re, state_s5_im, state_rwkv_S, state_rwkv_shift, state_gdn_S, state_gdn_conv, g_pre_mix, w_in, mlstm_gate_bias, mlstm_norm_g, s5_lam_re, s5_lam_im, s5_log_dt, s5_B_re, s5_B_im, s5_C_re, s5_C_im, s5_D, s5_w_glu, s5_b_glu, rwkv_mu, rwkv_w0, rwkv_w2, rwkv_a0, rwkv_a2, rwkv_g2, rwkv_k_k, rwkv_k_a, rwkv_r_k, rwkv_ln_w, rwkv_ln_b, gdn_conv_w, gdn_A_log, gdn_dt_bias, gdn_norm_g, w_out, g_post_mix, g_pre_ffn, w_gate, w_up, w_down, g_post_ffn):
    weights = (g_pre_mix, w_in, mlstm_gate_bias, mlstm_norm_g,
               s5_lam_re, s5_lam_im, s5_log_dt, s5_B_re, s5_B_im, s5_C_re, s5_C_im, s5_D, s5_w_glu, s5_b_glu,
               rwkv_mu, rwkv_w0, rwkv_w2, rwkv_a0, rwkv_a2, rwkv_g2, rwkv_k_k, rwkv_k_a, rwkv_r_k,
               rwkv_ln_w, rwkv_ln_b, gdn_conv_w, gdn_A_log, gdn_dt_bias, gdn_norm_g,
               w_out, g_post_mix, g_pre_ffn, w_gate, w_up, w_down, g_post_ffn)
    caches = (state_mlstm_C, state_mlstm_n, state_mlstm_m, state_s5_re, state_s5_im,
              state_rwkv_S, state_rwkv_shift, state_gdn_S, state_gdn_conv)
    yp, ys = x_prompt, x_sample
    outs_p, outs_s = [], []
    for l in range(DEPTH):
        q = _prep_layer({name: wt[l] for name, wt in zip(_PARAM_NAMES, weights)})
        yp, st_p = _layer(yp, q, _zero_state(x_prompt.shape[0]))
        ys, st_s = _layer(ys, q, tuple(c[l].astype(F32) for c in caches))
        outs_p.append(st_p)
        outs_s.append(st_s)
    stack = lambda outs: [jnp.stack(t) for t in zip(*outs)]
    return (yp, ys, *stack(outs_p), *stack(outs_s))
```

```python
import functools
import math

import jax
import jax.numpy as jnp
from jax import lax
from jax.experimental import pallas as pl
from jax.experimental.pallas import tpu as pltpu

F32 = jnp.float32
BF16 = jnp.bfloat16

D_MODEL = 2048
DEPTH = 2
EPS = 1e-6
GN_EPS = 64e-5
W_GRP = D_MODEL // 4
M_HEADS, M_DH = 4, 128
S5_CH, S5_GROUPS, S5_P = 16, 32, 64
S5_N = S5_GROUPS * S5_P
R_DH, R_HEADS = 64, 8
R_LR_W, R_LR_A, R_LR_G = 64, 64, 128
G_HEADS, G_DH = 4, 128
CONV_W = 4
D_FF = 5632
N_M = 4 * W_GRP + 2 * M_HEADS
N_S = W_GRP
N_R = 3 * W_GRP + R_LR_W + R_LR_A + R_LR_G
N_G = 4 * W_GRP + 2 * G_HEADS

COL_M = 0
COL_G = 2048
COL_R = 4096
COL_GATE = COL_R + N_R
COL_S = 6144
N_PROJ = 6656
GATE_MI, GATE_MF, GATE_GA, GATE_GB = 0, 4, 8, 12

CHUNK = 64
CHUNKS_PER_STEP = 1
LANES = 128
VMEM_LIMIT = 56 * 1024 * 1024


def _bf(x):
    return x.astype(BF16)


def _dot(a, b):
    return jnp.dot(_bf(a), _bf(b), preferred_element_type=F32)


def _dot_nt(a, b):
    return lax.dot_general(_bf(a), _bf(b), (((1,), (1,)), ((), ())), preferred_element_type=F32)


def _dot_tn(a, b):
    return lax.dot_general(_bf(a), _bf(b), (((0,), (0,)), ((), ())), preferred_element_type=F32)


def _split3(x):
    hi = _bf(x)
    r1 = x - hi.astype(F32)
    mid = _bf(r1)
    lo = _bf(r1 - mid.astype(F32))
    return hi, mid, lo


def _dot_exact_rhs(a_bf, x):
    hi, mid, lo = _split3(x)
    d = lambda v: jnp.dot(a_bf, v, preferred_element_type=F32)
    return d(hi) + d(mid) + d(lo)


def _dot_exact_lhs(x, b_bf):
    hi, mid, lo = _split3(x)
    d = lambda v: jnp.dot(v, b_bf, preferred_element_type=F32)
    return d(hi) + d(mid) + d(lo)


def _dot_nt_exact_rhs(a_bf, x):
    hi, mid, lo = _split3(x)
    d = lambda v: lax.dot_general(a_bf, v, (((1,), (1,)), ((), ())), preferred_element_type=F32)
    return d(hi) + d(mid) + d(lo)


def _sigmoid(x):
    return 1.0 / (1.0 + jnp.exp(-x))


def _softplus(x):
    return jnp.maximum(x, 0.0) + jnp.log(1.0 + jnp.exp(-jnp.abs(x)))


def _log_sigmoid(x):
    return -_softplus(-x)


def _silu(x):
    return x * _sigmoid(x)


def _gelu_tanh(x):
    return 0.5 * x * (1.0 + jnp.tanh(math.sqrt(2.0 / math.pi) * (x + 0.044715 * (x * x * x))))


def _rms(x, g):
    return x * lax.rsqrt(jnp.mean(x * x, -1, keepdims=True) + EPS) * g


def _iota2(shape, axis):
    return lax.broadcasted_iota(jnp.int32, shape, axis)


def _tri_masks(n):
    r, c = _iota2((n, n), 0), _iota2((n, n), 1)
    return r >= c, r > c


def _ones_where(mask):
    return jnp.where(mask, 1.0, 0.0).astype(BF16)


def _neumann_tail(n_mat, order):
    tm = n_mat
    p = n_mat
    k = 1
    while 2 * k < order:
        p = _dot(p, p)
        tm = tm + p + _dot(tm, p)
        k *= 2
        yield
    return tm


def _run_interleaved(gens):
    gens = list(gens)
    while gens:
        alive = []
        for g in gens:
            try:
                next(g)
                alive.append(g)
            except StopIteration:
                pass
        gens = alive


def _block_masks(n_blocks, L):
    n = n_blocks * L
    r, c = _iota2((n, n), 0), _iota2((n, n), 1)
    blk = lambda i: sum(jnp.where(i >= j * L, 1, 0) for j in range(1, n_blocks))
    same = blk(r) == blk(c)
    return same & (r >= c), same & (r > c)


def _col(x, j):
    return x[:, j:j + 1]


def _row(x, i):
    return x[i:i + 1, :]


def _params(sem):
    return pltpu.CompilerParams(dimension_semantics=sem, vmem_limit_bytes=VMEM_LIMIT)


def _proj_body(x_ref, g_ref, w_ref, o_ref, h_ref):
    @pl.when(pl.program_id(1) == 0)
    def _():
        h_ref[...] = _bf(_rms(x_ref[...], g_ref[...]))
    o_ref[...] = jnp.dot(h_ref[...], w_ref[...], preferred_element_type=F32)


def _proj(x2, g, w):
    n = x2.shape[0]
    tm = min(1024, n)
    tn = N_PROJ // 4
    return pl.pallas_call(
        _proj_body,
        grid=(n // tm, N_PROJ // tn),
        in_specs=[pl.BlockSpec((tm, D_MODEL), lambda i, j: (i, 0)),
                  pl.BlockSpec((1, D_MODEL), lambda i, j: (0, 0)),
                  pl.BlockSpec((D_MODEL, tn), lambda i, j: (0, j))],
        out_specs=pl.BlockSpec((tm, tn), lambda i, j: (i, j)),
        out_shape=jax.ShapeDtypeStruct((n, N_PROJ), F32),
        scratch_shapes=[pltpu.VMEM((tm, D_MODEL), BF16)],
        compiler_params=_params(("arbitrary", "arbitrary")),
        name="in_proj",
    )(x2, g, w)


def _mlstm_body(p_ref, gt_ref, gb_ref, ng_ref, c0_ref, n0_ref, m0_ref,
                y_ref, c_ref, n_ref, m_ref, *, bb, L):
    @pl.when(pl.program_id(1) == 0)
    def _():
        c_ref[...] = c0_ref[...]
        n_ref[...] = n0_ref[...]
        m_ref[...] = m0_ref[...]

    incl, _ = _tri_masks(L)
    tri_lo = _ones_where(incl)
    tri_up = _ones_where(_iota2((L, L), 0) <= _iota2((L, L), 1))
    sel = _ones_where(_iota2((8, LANES), 0) == _iota2((8, LANES), 1))
    lane = _iota2((1, LANES), 1)
    scale = M_DH ** -0.5

    heads, m_updates = [], []
    for b in range(bb):
        gates = gt_ref[b] + gb_ref[...]
        lf_cols = _log_sigmoid(gates)
        b_cols = _dot_exact_rhs(tri_lo, lf_cols)
        rows = _dot_nt_exact_rhs(sel, jnp.where(lane < GATE_MF, gates, lf_cols))
        b_rows = _dot_exact_lhs(rows, tri_up)
        m_row = m_ref[b]
        m_news = []

        def head(h, b=b, gates=gates, b_cols=b_cols, rows=rows, b_rows=b_rows, m_row=m_row, m_news=m_news):
            sl = slice(h * M_DH, (h + 1) * M_DH)
            q = p_ref[b, :, sl] * scale
            k = p_ref[b, :, W_GRP + h * M_DH:W_GRP + (h + 1) * M_DH]
            v = p_ref[b, :, 2 * W_GRP + h * M_DH:2 * W_GRP + (h + 1) * M_DH]
            o = p_ref[b, :, 3 * W_GRP + h * M_DH:3 * W_GRP + (h + 1) * M_DH]
            li_c = _col(gates, GATE_MI + h)
            b_c = _col(b_cols, GATE_MF + h)
            li_r = _row(rows, GATE_MI + h)
            b_r = _row(b_rows, GATE_MF + h)
            m_prev = jnp.sum(jnp.where(lane == h, m_row, 0.0), axis=1, keepdims=True)
            c_st = c_ref[b, h]
            n_st = n_ref[b, h:h + 1, :]

            log_d = jnp.where(incl, b_c - b_r + li_r, -jnp.inf)
            inter = b_c + m_prev
            mt = jnp.maximum(inter, jnp.max(log_d, axis=-1, keepdims=True))
            yield
            s = _dot_nt(q, k) * jnp.exp(log_d - mt)
            w_c = jnp.exp(inter - mt)
            yield
            num = _dot(s, v) + _dot(q, c_st) * w_c
            den = jnp.sum(s, -1, keepdims=True) + w_c * jnp.sum(q * n_st, -1, keepdims=True)
            hh = num / jnp.maximum(jnp.abs(den), jnp.exp(-mt))
            yield

            b_last = b_c[L - 1:L, :]
            tail = b_last - b_c + li_c
            m_new = jnp.maximum(b_last + m_prev, jnp.max(tail, axis=0, keepdims=True))
            wk = jnp.exp(tail - m_new)
            sc = jnp.exp(b_last + m_prev - m_new)
            c_ref[b, h] = sc * c_st + _dot_tn(k, wk * v)
            yield
            n_ref[b, h:h + 1, :] = sc * n_st + jnp.sum(wk * k, axis=0, keepdims=True)
            m_news.append((h, m_new))
            yield

            mu = jnp.mean(hh, -1, keepdims=True)
            hc = hh - mu
            hn = hc * lax.rsqrt(jnp.mean(hc * hc, -1, keepdims=True) + EPS)
            y_ref[b, :, sl] = _bf(_sigmoid(o) * (hn * ng_ref[:, sl]))

        heads.extend(head(h) for h in range(M_HEADS))
        m_updates.append((b, m_row, m_news))
    _run_interleaved(heads)
    for b, m_row, m_news in m_updates:
        m_out = m_row
        for h, m_new in m_news:
            m_out = jnp.where(lane == h, m_new, m_out)
        m_ref[b] = m_out


def _mlstm(proj3, gate_bias, norm_g, c0, n0, m0):
    B, T, _ = proj3.shape
    L = min(CHUNK, T)
    bb = 2
    st = lambda b, t: (b, 0, 0)
    return pl.pallas_call(
        functools.partial(_mlstm_body, bb=bb, L=L),
        grid=(B // bb, T // L),
        in_specs=[pl.BlockSpec((bb, L, 4 * W_GRP), lambda b, t: (b, t, COL_M // (4 * W_GRP))),
                  pl.BlockSpec((bb, L, LANES), lambda b, t: (b, t, COL_GATE // LANES)),
                  pl.BlockSpec((1, LANES), lambda b, t: (0, 0)),
                  pl.BlockSpec((1, W_GRP), lambda b, t: (0, 0)),
                  pl.BlockSpec((bb, M_HEADS, M_DH, M_DH), lambda b, t: (b, 0, 0, 0)),
                  pl.BlockSpec((bb, M_HEADS, M_DH), st),
                  pl.BlockSpec((bb, 1, LANES), st)],
        out_specs=[pl.BlockSpec((bb, L, W_GRP), lambda b, t: (b, t, 0)),
                   pl.BlockSpec((bb, M_HEADS, M_DH, M_DH), lambda b, t: (b, 0, 0, 0)),
                   pl.BlockSpec((bb, M_HEADS, M_DH), st),
                   pl.BlockSpec((bb, 1, LANES), st)],
        out_shape=[jax.ShapeDtypeStruct((B, T, W_GRP), BF16),
                   jax.ShapeDtypeStruct((B, M_HEADS, M_DH, M_DH), F32),
                   jax.ShapeDtypeStruct((B, M_HEADS, M_DH), F32),
                   jax.ShapeDtypeStruct((B, 1, LANES), F32)],
        compiler_params=_params(("arbitrary", "arbitrary")),
        name="mlstm",
    )(proj3, proj3, gate_bias, norm_g, c0, n0, m0)


def _gdn_body(p_ref, gt_ref, alog_ref, dtb_ref, cw_ref, ng_ref, s0_ref, cv0_ref,
              y_ref, s_ref, cv_ref, xp_ref, *, bb, L, Lb):
    t = pl.program_id(1)
    W3 = 3 * W_GRP
    PAD = 8
    H = G_HEADS

    @pl.when(t == 0)
    def _():
        s_ref[...] = s0_ref[...]
        xp_ref[:, PAD - (CONV_W - 1):PAD, :] = cv0_ref[...]

    incl, _ = _tri_masks(L)
    tri_lo = _ones_where(incl)
    incl_bd, strict_bd = _block_masks(H, L)
    sel = _ones_where(_iota2((8, LANES), 0) == _iota2((8, LANES), 1))
    lane = _iota2((1, LANES), 1)
    neg_a = -jnp.exp(alog_ref[...])
    rows_of = lambda x, h: x[h * L:(h + 1) * L]
    stack = lambda xs: jnp.concatenate(xs, axis=0)

    def chunk(b, c0, acc):
        gates = gt_ref[b, c0:c0 + L, :]
        g_cols = neg_a * _softplus(gates + dtb_ref[...])
        beta_cols = _sigmoid(gates)
        gc_cols = _dot_exact_rhs(tri_lo, g_cols)

        qs, ks, vs = [], [], []
        for h in range(H):
            q = acc[:, h * G_DH:(h + 1) * G_DH]
            k = acc[:, W_GRP + h * G_DH:W_GRP + (h + 1) * G_DH]
            qs.append(q * lax.rsqrt(jnp.sum(q * q, -1, keepdims=True) + 1e-6) * (G_DH ** -0.5))
            ks.append(k * lax.rsqrt(jnp.sum(k * k, -1, keepdims=True) + 1e-6))
            vs.append(acc[:, 2 * W_GRP + h * G_DH:2 * W_GRP + (h + 1) * G_DH])
        q_s, k_s, v_s = stack(qs), stack(ks), stack(vs)
        g_c = stack([_col(gc_cols, GATE_GA + h) for h in range(H)])
        beta = stack([_col(beta_cols, GATE_GB + h) for h in range(H)])
        g_r = _dot_nt_exact_rhs(sel, jnp.where(lane == 0, g_c, 0.0))[0:1, :]
        s_st = [s_ref[b, h] for h in range(H)]

        diff = g_c - g_r
        eg = jnp.exp(g_c)
        a_mat = beta * _dot_nt(k_s, k_s) * jnp.exp(jnp.where(strict_bd, diff, -jnp.inf))
        yield
        tm = yield from _neumann_tail(-a_mat, L)
        rhs = jnp.concatenate([beta * v_s, (beta * eg) * k_s], axis=1)
        sol = rhs + _dot(tm, rhs)
        yield
        u = sol[:, 0:G_DH] - stack([_dot(rows_of(sol, h)[:, G_DH:2 * G_DH], s_st[h]) for h in range(H)])
        qk = _dot_nt(q_s, k_s) * jnp.exp(jnp.where(incl_bd, diff, -jnp.inf))
        yield
        o_s = eg * stack([_dot(qs[h], s_st[h]) for h in range(H)]) + _dot(qk, u)
        yield

        for h in range(H):
            sl = slice(h * G_DH, (h + 1) * G_DH)
            g_ch = rows_of(g_c, h)
            g_last = g_ch[L - 1:L, :]
            s_ref[b, h] = jnp.exp(g_last) * s_st[h] + _dot_tn(ks[h] * jnp.exp(g_last - g_ch), rows_of(u, h))
            o = rows_of(o_s, h)
            o = o * lax.rsqrt(jnp.mean(o * o, -1, keepdims=True) + EPS) * ng_ref[...]
            z = p_ref[b, c0:c0 + L, W3 + h * G_DH:W3 + (h + 1) * G_DH]
            y_ref[b, c0:c0 + L, sl] = _bf(o * _silu(z))

    def stream(b, acc):
        for c0 in range(0, Lb, L):
            yield from chunk(b, c0, acc[c0:c0 + L])

    streams = []
    for b in range(bb):
        xp_ref[b, PAD:PAD + Lb, :] = p_ref[b, :, 0:W3]
        acc = xp_ref[b, PAD - 3:PAD - 3 + Lb, :] * cw_ref[0:1, :]
        for j in range(1, CONV_W):
            acc = acc + xp_ref[b, PAD - 3 + j:PAD - 3 + j + Lb, :] * cw_ref[j:j + 1, :]
        tail3 = xp_ref[b, PAD + Lb - (CONV_W - 1):PAD + Lb, :]
        xp_ref[b, PAD - (CONV_W - 1):PAD, :] = tail3
        cv_ref[b] = tail3
        streams.append(stream(b, _silu(acc)))
    _run_interleaved(streams)


def _gdn(proj3, alog_row, dtb_row, conv_w, norm_g, s0, cv0):
    B, T, _ = proj3.shape
    L = min(CHUNK, T)
    Lb = min(CHUNKS_PER_STEP * L, T)
    bb = 2
    return pl.pallas_call(
        functools.partial(_gdn_body, bb=bb, L=L, Lb=Lb),
        grid=(B // bb, T // Lb),
        in_specs=[pl.BlockSpec((bb, Lb, 4 * W_GRP), lambda b, t: (b, t, COL_G // (4 * W_GRP))),
                  pl.BlockSpec((bb, Lb, LANES), lambda b, t: (b, t, COL_GATE // LANES)),
                  pl.BlockSpec((1, LANES), lambda b, t: (0, 0)),
                  pl.BlockSpec((1, LANES), lambda b, t: (0, 0)),
                  pl.BlockSpec((CONV_W, 3 * W_GRP), lambda b, t: (0, 0)),
                  pl.BlockSpec((1, G_DH), lambda b, t: (0, 0)),
                  pl.BlockSpec((bb, G_HEADS, G_DH, G_DH), lambda b, t: (b, 0, 0, 0)),
                  pl.BlockSpec((bb, CONV_W - 1, 3 * W_GRP), lambda b, t: (b, 0, 0))],
        out_specs=[pl.BlockSpec((bb, Lb, W_GRP), lambda b, t: (b, t, 0)),
                   pl.BlockSpec((bb, G_HEADS, G_DH, G_DH), lambda b, t: (b, 0, 0, 0)),
                   pl.BlockSpec((bb, CONV_W - 1, 3 * W_GRP), lambda b, t: (b, 0, 0))],
        out_shape=[jax.ShapeDtypeStruct((B, T, W_GRP), BF16),
                   jax.ShapeDtypeStruct((B, G_HEADS, G_DH, G_DH), F32),
                   jax.ShapeDtypeStruct((B, CONV_W - 1, 3 * W_GRP), F32)],
        scratch_shapes=[pltpu.VMEM((bb, 8 + Lb, 3 * W_GRP), F32)],
        compiler_params=_params(("arbitrary", "arbitrary")),
        name="gdn",
    )(proj3, proj3, alog_row, dtb_row, conv_w, norm_g, s0, cv0)


def _rwkv_body(p_ref, mu_ref, wwa_ref, w0_ref, a0_ref, g2_ref, kk_ref, ka_ref, rk_ref,
               lnw_ref, lnb_ref, s0_ref, sh0_ref,
               y_ref, s_ref, sh_ref, xs_ref, *, bb, L, Lb):
    t = pl.program_id(1)
    PAD = 8
    NQ = 4

    @pl.when(t == 0)
    def _():
        s_ref[...] = s0_ref[...]
        xs_ref[:, PAD - 1:PAD, :] = sh0_ref[...]

    incl, _ = _tri_masks(L)
    tri_lo = _ones_where(incl)
    incl_bd, strict_bd = _block_masks(NQ, L)
    lane = _iota2((1, LANES), 1)
    head_masks = (lane < R_DH, lane >= R_DH)
    r_i, c_i = _iota2((LANES, LANES), 0), _iota2((LANES, LANES), 1)
    same_head = (r_i < R_DH) == (c_i < R_DH)
    seg = _ones_where(same_head)
    seg_sum = lambda x: _dot_exact_lhs(x, seg)
    rows_of = lambda x, i: x[i * L:(i + 1) * L]
    stack = lambda xs: jnp.concatenate(xs, axis=0)

    groups = []
    for b in range(bb):
        xs_ref[b, PAD:PAD + Lb, :] = p_ref[b, :, 0:N_R]
        cur = p_ref[b, :, 0:N_R]
        prev = xs_ref[b, PAD - 1:PAD - 1 + Lb, :]
        last = xs_ref[b, PAD + Lb - 1:PAD + Lb, :]
        xs_ref[b, PAD - 1:PAD, :] = last
        sh_ref[b] = last

        xm = cur + (prev - cur) * mu_ref[...]
        r_all = xm[:, 0:W_GRP]
        k_all = xm[:, W_GRP:2 * W_GRP]
        v_all = xm[:, 2 * W_GRP:3 * W_GRP]
        wa_code = xm[:, 3 * W_GRP:3 * W_GRP + LANES]
        g_code = xm[:, 3 * W_GRP + LANES:3 * W_GRP + 2 * LANES]
        wa_in = jnp.where(lane < R_LR_W, jnp.tanh(wa_code), wa_code)
        wa = _dot(wa_in, wwa_ref[...])
        w_all = -_softplus(-(w0_ref[...] + wa[:, 0:W_GRP])) - 0.5
        ld_all = -jnp.exp(w_all)
        a_all = _sigmoid(a0_ref[...] + wa[:, W_GRP:2 * W_GRP])
        g_all = _dot(_sigmoid(g_code), g2_ref[...])
        k2_all = k_all * (1.0 + (a_all - 1.0) * ka_ref[...])
        kkr_all = k_all * kk_ref[...]

        def quad(qd, c0, b=b, r_all=r_all, k2_all=k2_all, v_all=v_all, ld_all=ld_all, a_all=a_all,
                 kkr_all=kkr_all, g_all=g_all):
            rs = slice(c0, c0 + L)
            pairs = (2 * qd, 2 * qd + 1)
            pr = {}
            for p in pairs:
                sl = slice(p * LANES, (p + 1) * LANES)
                r, k, v, ld, a, kkr = (r_all[rs, sl], k2_all[rs, sl], v_all[rs, sl], ld_all[rs, sl], a_all[rs, sl],
                                       kkr_all[rs, sl])
                kk = kkr * lax.rsqrt(seg_sum(kkr * kkr) + 1e-6)
                cum = _dot_exact_rhs(tri_lo, ld)
                e_neg = jnp.exp(-cum)
                pr[p] = dict(r=r, k=k, v=v, cum=cum, a_t=jnp.exp(cum - ld) * kk, b_t=-(kk * a) * e_neg,
                             k_t=k * e_neg, r_t=r * jnp.exp(cum), s=s_ref[b, p])
            slots = [(p, hm) for p in pairs for hm in head_masks]
            a_s = stack([jnp.where(hm, pr[p]['a_t'], 0.0) for p, hm in slots])
            r_s = stack([jnp.where(hm, pr[p]['r_t'], 0.0) for p, hm in slots])
            b_s = stack([pr[p]['b_t'] for p, _ in slots])
            k_s = stack([pr[p]['k_t'] for p, _ in slots])
            v_s = stack([pr[p]['v'] for p, _ in slots])
            a_ab = jnp.where(strict_bd, _dot_nt(a_s, b_s), 0.0)
            a_ak = jnp.where(strict_bd, _dot_nt(a_s, k_s), 0.0)
            a_rb = jnp.where(incl_bd, _dot_nt(r_s, b_s), 0.0)
            a_rk = jnp.where(incl_bd, _dot_nt(r_s, k_s), 0.0)
            yield
            tm = yield from _neumann_tail(a_ab, L)
            rhs = stack([_dot_nt(rows_of(a_s, i), pr[p]['s']) for i, (p, _) in enumerate(slots)]) + _dot(a_ak, v_s)
            yield
            u_s = rhs + _dot(tm, rhs)
            yield
            y_s = _dot(a_rb, u_s) + _dot(a_rk, v_s)
            yield

            for j, p in enumerate(pairs):
                sl = slice(p * LANES, (p + 1) * LANES)
                d = pr[p]
                u = jnp.where(head_masks[0], rows_of(u_s, 2 * j), rows_of(u_s, 2 * j + 1))
                y = _dot_nt(d['r_t'], d['s']) + jnp.where(head_masks[0], rows_of(y_s, 2 * j), rows_of(y_s, 2 * j + 1))
                w_last = jnp.exp(d['cum'][L - 1:L, :])
                s_new = (d['s'] + _dot_tn(u, d['b_t']) + _dot_tn(d['v'], d['k_t'])) * w_last
                s_ref[b, p] = jnp.where(same_head, s_new, 0.0)

                mean = seg_sum(y) * (1.0 / R_DH)
                yc = y - mean
                var = seg_sum(yc * yc) * (1.0 / R_DH)
                yn = yc * lax.rsqrt(var + GN_EPS) * lnw_ref[:, sl] + lnb_ref[:, sl]
                bonus = seg_sum(d['r'] * d['k'] * rk_ref[:, sl]) * d['v']
                y_ref[b, rs, sl] = _bf((yn + bonus) * g_all[rs, sl])
                yield

        def group(qd, quad=quad):
            for c0 in range(0, Lb, L):
                yield from quad(qd, c0)

        groups.extend(group(qd) for qd in range(R_HEADS // NQ))
    _run_interleaved(groups)


def _rwkv(proj3, mu, wwa, w0, a0, g2, k_k, k_a, r_k, ln_w, ln_b, s0, sh0):
    B, T, _ = proj3.shape
    L = min(CHUNK, T)
    Lb = min(CHUNKS_PER_STEP * L, T)
    bb = 2
    NP = R_HEADS // 2
    vec = lambda n: pl.BlockSpec((1, n), lambda b, t: (0, 0))
    return pl.pallas_call(
        functools.partial(_rwkv_body, bb=bb, L=L, Lb=Lb),
        grid=(B // bb, T // Lb),
        in_specs=[pl.BlockSpec((bb, Lb, 4 * W_GRP), lambda b, t: (b, t, COL_R // (4 * W_GRP))),
                  vec(N_R),
                  pl.BlockSpec((LANES, 2 * W_GRP), lambda b, t: (0, 0)),
                  vec(W_GRP), vec(W_GRP),
                  pl.BlockSpec((R_LR_G, W_GRP), lambda b, t: (0, 0)),
                  vec(W_GRP), vec(W_GRP), vec(W_GRP), vec(W_GRP), vec(W_GRP),
                  pl.BlockSpec((bb, NP, LANES, LANES), lambda b, t: (b, 0, 0, 0)),
                  pl.BlockSpec((bb, 1, N_R), lambda b, t: (b, 0, 0))],
        out_specs=[pl.BlockSpec((bb, Lb, W_GRP), lambda b, t: (b, t, 0)),
                   pl.BlockSpec((bb, NP, LANES, LANES), lambda b, t: (b, 0, 0, 0)),
                   pl.BlockSpec((bb, 1, N_R), lambda b, t: (b, 0, 0))],
        out_shape=[jax.ShapeDtypeStruct((B, T, W_GRP), BF16),
                   jax.ShapeDtypeStruct((B, NP, LANES, LANES), F32),
                   jax.ShapeDtypeStruct((B, 1, N_R), F32)],
        scratch_shapes=[pltpu.VMEM((bb, 8 + Lb, N_R), F32)],
        compiler_params=_params(("arbitrary", "arbitrary")),
        name="rwkv7",
    )(proj3, mu, wwa, w0, a0, g2, k_k, k_a, r_k, ln_w, ln_b, s0, sh0)


def _s5_body(u_ref, win_ref, wout_ref, lam_ref, pw_ref, d_ref, wglu_ref, bglu_ref, hr0_ref, hi0_ref,
             y_ref, hr_ref, hi_ref, sr_ref, si_ref, *, Lb):
    @pl.when(pl.program_id(1) == 0)
    def _():
        hr_ref[...] = hr0_ref[...]
        hi_ref[...] = hi0_ref[...]

    NB = S5_N // W_GRP
    u = u_ref[0]
    for c in range(NB):
        bu = _dot(u[:, c * LANES:(c + 1) * LANES], win_ref[c])
        sr_ref[:, c * W_GRP:(c + 1) * W_GRP] = bu[:, 0:W_GRP]
        si_ref[:, c * W_GRP:(c + 1) * W_GRP] = bu[:, W_GRP:2 * W_GRP]

    row8 = _iota2((8, S5_N), 0)
    lam_r = [lam_ref[k:k + 1, :] for k in range(3)]
    lam_i = [lam_ref[3 + k:4 + k, :] for k in range(3)]
    pw_r = pw_ref[0:8, :]
    pw_i = pw_ref[8:16, :]

    def tile(n, carry):
        cr, ci = carry
        base = pl.multiple_of(n * 8, 8)
        xr = sr_ref[pl.ds(base, 8), :]
        xi = si_ref[pl.ds(base, 8), :]
        for lvl, sft in enumerate((1, 2, 4)):
            keep = row8 >= sft
            zr = jnp.where(keep, pltpu.roll(xr, sft, axis=0), 0.0)
            zi = jnp.where(keep, pltpu.roll(xi, sft, axis=0), 0.0)
            xr, xi = (xr + lam_r[lvl] * zr - lam_i[lvl] * zi,
                      xi + lam_r[lvl] * zi + lam_i[lvl] * zr)
        xr, xi = xr + pw_r * cr - pw_i * ci, xi + pw_r * ci + pw_i * cr
        sr_ref[pl.ds(base, 8), :] = xr
        si_ref[pl.ds(base, 8), :] = xi
        return xr[7:8, :], xi[7:8, :]

    cr, ci = lax.fori_loop(0, Lb // 8, tile, (hr_ref[0], hi_ref[0]))
    hr_ref[0] = cr
    hi_ref[0] = ci

    ys = []
    for c in range(NB):
        sl = slice(c * W_GRP, (c + 1) * W_GRP)
        ys.append(_dot(sr_ref[:, sl], wout_ref[c, 0]) + _dot(si_ref[:, sl], wout_ref[c, 1]))
    y = jnp.concatenate(ys, axis=-1) + d_ref[...] * u
    y = _gelu_tanh(y)
    y_ref[0] = _bf(y * _sigmoid(_dot(y, wglu_ref[...]) + bglu_ref[...]))


def _s5(proj3, win, wout, lam_pows, row_pows, d_vec, w_glu, b_glu, hr0, hi0):
    B, T, _ = proj3.shape
    Lb = min(512, T)
    st = pl.BlockSpec((1, 1, S5_N), lambda b, t: (b, 0, 0))
    full = lambda a: pl.BlockSpec(a.shape, lambda b, t: (0,) * a.ndim)
    return pl.pallas_call(
        functools.partial(_s5_body, Lb=Lb),
        grid=(B, T // Lb),
        in_specs=[pl.BlockSpec((1, Lb, W_GRP), lambda b, t: (b, t, COL_S // W_GRP)),
                  full(win), full(wout), full(lam_pows), full(row_pows), full(d_vec),
                  full(w_glu), full(b_glu), st, st],
        out_specs=[pl.BlockSpec((1, Lb, W_GRP), lambda b, t: (b, t, 0)), st, st],
        out_shape=[jax.ShapeDtypeStruct((B, T, W_GRP), BF16),
                   jax.ShapeDtypeStruct((B, 1, S5_N), F32),
                   jax.ShapeDtypeStruct((B, 1, S5_N), F32)],
        scratch_shapes=[pltpu.VMEM((Lb, S5_N), F32), pltpu.VMEM((Lb, S5_N), F32)],
        compiler_params=_params(("arbitrary", "arbitrary")),
        name="s5",
    )(proj3, win, wout, lam_pows, row_pows, d_vec, w_glu, b_glu, hr0, hi0)


def _ffn_body(x_ref, ym_ref, ys_ref, yr_ref, yg_ref, wo_ref, gpm_ref, gpf_ref, wg_ref, wu_ref, wd_ref, gpo_ref,
              o_ref, hf_ref, acc_ref):
    f = pl.program_id(1)

    @pl.when(f == 0)
    def _():
        mix = jnp.dot(ym_ref[...], wo_ref[0], preferred_element_type=F32)
        mix = mix + jnp.dot(ys_ref[...], wo_ref[1], preferred_element_type=F32)
        mix = mix + jnp.dot(yr_ref[...], wo_ref[2], preferred_element_type=F32)
        mix = mix + jnp.dot(yg_ref[...], wo_ref[3], preferred_element_type=F32)
        x1 = x_ref[...] + _rms(mix, gpm_ref[...])
        o_ref[...] = x1
        hf_ref[...] = _bf(_rms(x1, gpf_ref[...]))
        acc_ref[...] = jnp.zeros_like(acc_ref)

    hf = hf_ref[...]
    gate = jnp.dot(hf, wg_ref[...], preferred_element_type=F32)
    up = jnp.dot(hf, wu_ref[...], preferred_element_type=F32)
    acc_ref[...] += jnp.dot(_bf(_silu(gate) * up), wd_ref[...], preferred_element_type=F32)

    @pl.when(f == pl.num_programs(1) - 1)
    def _():
        o_ref[...] = o_ref[...] + _rms(acc_ref[...], gpo_ref[...])


def _out_ffn(x2, ym, ys, yr, yg, w_out, g_post_mix, g_pre_ffn, w_gate, w_up, w_down, g_post_ffn):
    n = x2.shape[0]
    tm = min(512, n)
    tf = 512
    row = lambda w: pl.BlockSpec((tm, w), lambda i, f: (i, 0))
    vec = pl.BlockSpec((1, D_MODEL), lambda i, f: (0, 0))
    return pl.pallas_call(
        _ffn_body,
        grid=(n // tm, D_FF // tf),
        in_specs=[row(D_MODEL), row(W_GRP), row(W_GRP), row(W_GRP), row(W_GRP),
                  pl.BlockSpec((4, W_GRP, D_MODEL), lambda i, f: (0, 0, 0), pipeline_mode=pl.Buffered(1)),
                  vec, vec,
                  pl.BlockSpec((D_MODEL, tf), lambda i, f: (0, f)),
                  pl.BlockSpec((D_MODEL, tf), lambda i, f: (0, f)),
                  pl.BlockSpec((tf, D_MODEL), lambda i, f: (f, 0)),
                  vec],
        out_specs=row(D_MODEL),
        out_shape=jax.ShapeDtypeStruct((n, D_MODEL), F32),
        scratch_shapes=[pltpu.VMEM((tm, D_MODEL), BF16), pltpu.VMEM((tm, D_MODEL), F32)],
        compiler_params=_params(("arbitrary", "arbitrary")),
        name="out_ffn",
    )(x2, ym, ys, yr, yg, w_out, g_post_mix, g_pre_ffn, w_gate, w_up, w_down, g_post_ffn)


def _lane_row(pairs):
    row = jnp.zeros((LANES,), F32)
    for off, vals in pairs:
        row = lax.dynamic_update_slice(row, vals.astype(F32), (off,))
    return row[None, :]


def _block_diag(blocks):
    n, r, c = blocks.shape
    eye = jnp.eye(n, dtype=blocks.dtype)
    return (eye[:, None, :, None] * blocks[:, :, None, :]).reshape(n * r, n * c)


def _complex_pow(re, im, n):
    pr, pi = re, im
    for _ in range(n - 1):
        pr, pi = pr * re - pi * im, pr * im + pi * re
    return pr, pi


def _prep_layer(p):
    f32 = lambda a: a.astype(F32)
    w_in = p['w_in']
    off_s = N_M
    off_r = N_M + N_S
    off_g = N_M + N_S + N_R
    gate_cols = jnp.concatenate([w_in[:, 4 * W_GRP:N_M], w_in[:, off_g + 4 * W_GRP:off_g + N_G]], axis=1)
    w_perm = jnp.concatenate([
        w_in[:, 0:4 * W_GRP],
        w_in[:, off_g:off_g + 4 * W_GRP],
        w_in[:, off_r:off_r + N_R],
        gate_cols,
        jnp.zeros((D_MODEL, COL_S - COL_GATE - 16), w_in.dtype),
        w_in[:, off_s:off_s + N_S]], axis=1)
    q = {'w_in': _bf(w_perm)}
    q['g_pre_mix'] = f32(p['g_pre_mix'])[None, :]

    gb = f32(p['mlstm_gate_bias'])
    q['m_gate_bias'] = _lane_row([(GATE_MI, gb[0]), (GATE_MF, gb[1])])
    q['m_norm_g'] = f32(p['mlstm_norm_g'])[None, :]

    lam_re, lam_im = f32(p['s5_lam_re']), f32(p['s5_lam_im'])
    dt = jnp.exp(f32(p['s5_log_dt']))[:, None]
    mag = jnp.exp(lam_re * dt)
    lb_re = mag * jnp.cos(lam_im * dt)
    lb_im = mag * jnp.sin(lam_im * dt)
    nr = lb_re - 1.0
    den = lam_re * lam_re + lam_im * lam_im
    f_re = (nr * lam_re + lb_im * lam_im) / den
    f_im = (lb_im * lam_re - nr * lam_im) / den
    B_re, B_im = f32(p['s5_B_re']), f32(p['s5_B_im'])
    Bb_re = f_re[..., None] * B_re - f_im[..., None] * B_im
    Bb_im = f_re[..., None] * B_im + f_im[..., None] * B_re
    nb, gpb = S5_N // W_GRP, S5_GROUPS // (S5_N // W_GRP)
    bd_in = lambda m: jnp.stack([_block_diag(jnp.swapaxes(m, 1, 2)[c * gpb:(c + 1) * gpb]) for c in range(nb)])
    q['s5_win'] = _bf(jnp.concatenate([bd_in(Bb_re), bd_in(Bb_im)], axis=-1))
    bd_out = lambda m: jnp.stack([_block_diag(jnp.swapaxes(m, 1, 2)[c * gpb:(c + 1) * gpb]) for c in range(nb)])
    q['s5_wout'] = _bf(jnp.stack([bd_out(f32(p['s5_C_re'])), -bd_out(f32(p['s5_C_im']))], axis=1))
    lr, li = lb_re.reshape(1, S5_N), lb_im.reshape(1, S5_N)
    pows = [_complex_pow(lr, li, n) for n in range(1, 9)]
    zero = jnp.zeros((2, S5_N), F32)
    q['s5_lam_pows'] = jnp.concatenate([pows[0][0], pows[1][0], pows[3][0], pows[0][1], pows[1][1], pows[3][1], zero], 0)
    q['s5_row_pows'] = jnp.concatenate([pw[0] for pw in pows] + [pw[1] for pw in pows], 0)
    q['s5_D'] = f32(p['s5_D'])[None, :]
    q['s5_w_glu'] = _bf(p['s5_w_glu'])
    q['s5_b_glu'] = f32(p['s5_b_glu'])[None, :]

    q['r_mu'] = f32(p['rwkv_mu'])[None, :]
    zw = jnp.zeros((R_LR_W, W_GRP), F32)
    q['r_wwa'] = _bf(jnp.concatenate([jnp.concatenate([f32(p['rwkv_w2']), zw], 1),
                                      jnp.concatenate([zw, f32(p['rwkv_a2'])], 1)], 0))
    q['r_w0'] = f32(p['rwkv_w0'])[None, :]
    q['r_a0'] = f32(p['rwkv_a0'])[None, :]
    q['r_g2'] = _bf(p['rwkv_g2'])
    q['r_k_k'] = f32(p['rwkv_k_k'])[None, :]
    q['r_k_a'] = f32(p['rwkv_k_a'])[None, :]
    q['r_r_k'] = f32(p['rwkv_r_k']).reshape(1, W_GRP)
    q['r_ln_w'] = f32(p['rwkv_ln_w'])[None, :]
    q['r_ln_b'] = f32(p['rwkv_ln_b'])[None, :]

    q['g_alog'] = _lane_row([(GATE_GA, f32(p['gdn_A_log']))])
    q['g_dtb'] = _lane_row([(GATE_GA, f32(p['gdn_dt_bias']))])
    q['g_conv_w'] = f32(p['gdn_conv_w'])
    q['g_norm_g'] = f32(p['gdn_norm_g'])[None, :]

    q['w_out'] = _bf(p['w_out']).reshape(4, W_GRP, D_MODEL)
    q['g_post_mix'] = f32(p['g_post_mix'])[None, :]
    q['g_pre_ffn'] = f32(p['g_pre_ffn'])[None, :]
    q['w_gate'] = _bf(p['w_gate'])
    q['w_up'] = _bf(p['w_up'])
    q['w_down'] = _bf(p['w_down'])
    q['g_post_ffn'] = f32(p['g_post_ffn'])[None, :]
    return q


def _pack_rwkv_state(s):
    B = s.shape[0]
    s = s.reshape(B, R_HEADS // 2, 2, R_DH, R_DH)
    z = jnp.zeros_like(s[:, :, 0])
    top = jnp.concatenate([s[:, :, 0], z], axis=-1)
    bot = jnp.concatenate([z, s[:, :, 1]], axis=-1)
    return jnp.concatenate([top, bot], axis=-2)


def _unpack_rwkv_state(s):
    B = s.shape[0]
    a = s[:, :, :R_DH, :R_DH]
    d = s[:, :, R_DH:, R_DH:]
    return jnp.stack([a, d], axis=2).reshape(B, R_HEADS, R_DH, R_DH)


def _layer(x, q, state):
    mC0, mn0, mm0, s5r0, s5i0, rS0, rsh0, gS0, gcv0 = state
    B, T, _ = x.shape
    x2 = x.reshape(B * T, D_MODEL)
    proj3 = _proj(x2, q['g_pre_mix'], q['w_in']).reshape(B, T, N_PROJ)

    mm0p = jnp.pad(mm0, ((0, 0), (0, LANES - M_HEADS)))[:, None, :]
    y_m, mC, mn, mmp = _mlstm(proj3, q['m_gate_bias'], q['m_norm_g'], mC0, mn0, mm0p)
    mm = mmp[:, 0, :M_HEADS]

    y_s, s5r, s5i = _s5(proj3, q['s5_win'], q['s5_wout'], q['s5_lam_pows'], q['s5_row_pows'], q['s5_D'],
                        q['s5_w_glu'], q['s5_b_glu'],
                        s5r0.reshape(B, 1, S5_N), s5i0.reshape(B, 1, S5_N))
    s5r = s5r.reshape(B, S5_GROUPS, S5_P)
    s5i = s5i.reshape(B, S5_GROUPS, S5_P)

    y_r, rSp, rsh = _rwkv(proj3, q['r_mu'], q['r_wwa'], q['r_w0'], q['r_a0'], q['r_g2'], q['r_k_k'], q['r_k_a'],
                          q['r_r_k'], q['r_ln_w'], q['r_ln_b'], _pack_rwkv_state(rS0), rsh0[:, None, :])
    rS = _unpack_rwkv_state(rSp)
    rsh = rsh[:, 0, :]

    y_g, gS, gcv = _gdn(proj3, q['g_alog'], q['g_dtb'], q['g_conv_w'], q['g_norm_g'], gS0, gcv0)

    w = lambda a: a.reshape(B * T, W_GRP)
    out = _out_ffn(x2, w(y_m), w(y_s), w(y_r), w(y_g), q['w_out'], q['g_post_mix'], q['g_pre_ffn'],
                   q['w_gate'], q['w_up'], q['w_down'], q['g_post_ffn'])
    return out.reshape(B, T, D_MODEL), (mC, mn, mm, s5r, s5i, rS, rsh, gS, gcv)


def _zero_state(b):
    z = lambda *s: jnp.zeros(s, F32)
    return (z(b, M_HEADS, M_DH, M_DH), z(b, M_HEADS, M_DH), z(b, M_HEADS),
            z(b, S5_GROUPS, S5_P), z(b, S5_GROUPS, S5_P),
            z(b, R_HEADS, R_DH, R_DH), z(b, N_R),
            z(b, G_HEADS, G_DH, G_DH), z(b, CONV_W - 1, 3 * W_GRP))


_PARAM_NAMES = ('g_pre_mix', 'w_in', 'mlstm_gate_bias', 'mlstm_norm_g',
                's5_lam_re', 's5_lam_im', 's5_log_dt', 's5_B_re', 's5_B_im', 's5_C_re', 's5_C_im',
                's5_D', 's5_w_glu', 's5_b_glu',
                'rwkv_mu', 'rwkv_w0', 'rwkv_w2', 'rwkv_a0', 'rwkv_a2', 'rwkv_g2', 'rwkv_k_k', 'rwkv_k_a',
                'rwkv_r_k', 'rwkv_ln_w', 'rwkv_ln_b',
                'gdn_conv_w', 'gdn_A_log', 'gdn_dt_bias', 'gdn_norm_g',
                'w_out', 'g_post_mix', 'g_pre_ffn', 'w_gate', 'w_up', 'w_down', 'g_post_ffn')


def kernel(x_prompt, x_sample, state_mlstm_C, state_mlstm_n, state_mlstm_m, state_s5_re, state_s5_im, state_rwkv_S, state_rwkv_shift, state_gdn_S, state_gdn_conv, g_pre_mix, w_in, mlstm_gate_bias, mlstm_norm_g, s5_lam_re, s5_lam_im, s5_log_dt, s5_B_re, s5_B_im, s5_C_re, s5_C_im, s5_D, s5_w_glu, s5_b_glu, rwkv_mu, rwkv_w0, rwkv_w2, rwkv_a0, rwkv_a2, rwkv_g2, rwkv_k_k, rwkv_k_a, rwkv_r_k, rwkv_ln_w, rwkv_ln_b, gdn_conv_w, gdn_A_log, gdn_dt_bias, gdn_norm_g, w_out, g_post_mix, g_pre_ffn, w_gate, w_up, w_down, g_post_ffn):
    weights = (g_pre_mix, w_in, mlstm_gate_bias, mlstm_norm_g,
               s5_lam_re, s5_lam_im, s5_log_dt, s5_B_re, s5_B_im, s5_C_re, s5_C_im, s5_D, s5_w_glu, s5_b_glu,
               rwkv_mu, rwkv_w0, rwkv_w2, rwkv_a0, rwkv_a2, rwkv_g2, rwkv_k_k, rwkv_k_a, rwkv_r_k,
               rwkv_ln_w, rwkv_ln_b, gdn_conv_w, gdn_A_log, gdn_dt_bias, gdn_norm_g,
               w_out, g_post_mix, g_pre_ffn, w_gate, w_up, w_down, g_post_ffn)
    caches = (state_mlstm_C, state_mlstm_n, state_mlstm_m, state_s5_re, state_s5_im,
              state_rwkv_S, state_rwkv_shift, state_gdn_S, state_gdn_conv)
    yp, ys = x_prompt, x_sample
    outs_p, outs_s = [], []
    for l in range(DEPTH):
        q = _prep_layer({name: wt[l] for name, wt in zip(_PARAM_NAMES, weights)})
        yp, st_p = _layer(yp, q, _zero_state(x_prompt.shape[0]))
        ys, st_s = _layer(ys, q, tuple(c[l].astype(F32) for c in caches))
        outs_p.append(st_p)
        outs_s.append(st_s)
    stack = lambda outs: [jnp.stack(t) for t in zip(*outs)]
    return (yp, ys, *stack(outs_p), *stack(outs_s))
```

```python
import functools
import math

import jax
import jax.numpy as jnp
from jax import lax
from jax.experimental import pallas as pl
from jax.experimental.pallas import tpu as pltpu

F32 = jnp.float32
BF16 = jnp.bfloat16

D_MODEL = 2048
DEPTH = 2
EPS = 1e-6
GN_EPS = 64e-5
W_GRP = D_MODEL // 4
M_HEADS, M_DH = 4, 128
S5_CH, S5_GROUPS, S5_P = 16, 32, 64
S5_N = S5_GROUPS * S5_P
R_DH, R_HEADS = 64, 8
R_LR_W, R_LR_A, R_LR_G = 64, 64, 128
G_HEADS, G_DH = 4, 128
CONV_W = 4
D_FF = 5632
N_M = 4 * W_GRP + 2 * M_HEADS
N_S = W_GRP
N_R = 3 * W_GRP + R_LR_W + R_LR_A + R_LR_G
N_G = 4 * W_GRP + 2 * G_HEADS

COL_M = 0
COL_G = 2048
COL_R = 4096
COL_GATE = COL_R + N_R
COL_S = 6144
N_PROJ = 6656
GATE_MI, GATE_MF, GATE_GA, GATE_GB = 0, 4, 8, 12

CHUNK = 64
CHUNKS_PER_STEP = 1
LANES = 128
VMEM_LIMIT = 56 * 1024 * 1024


def _bf(x):
    return x.astype(BF16)


def _dot(a, b):
    return jnp.dot(_bf(a), _bf(b), preferred_element_type=F32)


def _dot_nt(a, b):
    return lax.dot_general(_bf(a), _bf(b), (((1,), (1,)), ((), ())), preferred_element_type=F32)


def _dot_tn(a, b):
    return lax.dot_general(_bf(a), _bf(b), (((0,), (0,)), ((), ())), preferred_element_type=F32)


def _split3(x):
    hi = _bf(x)
    r1 = x - hi.astype(F32)
    mid = _bf(r1)
    lo = _bf(r1 - mid.astype(F32))
    return hi, mid, lo


def _dot_exact_rhs(a_bf, x):
    hi, mid, lo = _split3(x)
    d = lambda v: jnp.dot(a_bf, v, preferred_element_type=F32)
    return d(hi) + d(mid) + d(lo)


def _dot_exact_lhs(x, b_bf):
    hi, mid, lo = _split3(x)
    d = lambda v: jnp.dot(v, b_bf, preferred_element_type=F32)
    return d(hi) + d(mid) + d(lo)


def _dot_nt_exact_rhs(a_bf, x):
    hi, mid, lo = _split3(x)
    d = lambda v: lax.dot_general(a_bf, v, (((1,), (1,)), ((), ())), preferred_element_type=F32)
    return d(hi) + d(mid) + d(lo)


def _sigmoid(x):
    return 1.0 / (1.0 + jnp.exp(-x))


def _softplus(x):
    return jnp.maximum(x, 0.0) + jnp.log(1.0 + jnp.exp(-jnp.abs(x)))


def _log_sigmoid(x):
    return -_softplus(-x)


def _silu(x):
    return x * _sigmoid(x)


def _gelu_tanh(x):
    return 0.5 * x * (1.0 + jnp.tanh(math.sqrt(2.0 / math.pi) * (x + 0.044715 * (x * x * x))))


def _rms(x, g):
    return x * lax.rsqrt(jnp.mean(x * x, -1, keepdims=True) + EPS) * g


def _iota2(shape, axis):
    return lax.broadcasted_iota(jnp.int32, shape, axis)


def _tri_masks(n):
    r, c = _iota2((n, n), 0), _iota2((n, n), 1)
    return r >= c, r > c


def _ones_where(mask):
    return jnp.where(mask, 1.0, 0.0).astype(BF16)


def _neumann_tail(n_mat, order):
    tm = n_mat
    p = n_mat
    k = 1
    while 2 * k < order:
        p = _dot(p, p)
        tm = tm + p + _dot(tm, p)
        k *= 2
        yield
    return tm


def _run_interleaved(gens):
    gens = list(gens)
    while gens:
        alive = []
        for g in gens:
            try:
                next(g)
                alive.append(g)
            except StopIteration:
                pass
        gens = alive


def _block_masks(n_blocks, L):
    n = n_blocks * L
    r, c = _iota2((n, n), 0), _iota2((n, n), 1)
    blk = lambda i: sum(jnp.where(i >= j * L, 1, 0) for j in range(1, n_blocks))
    same = blk(r) == blk(c)
    return same & (r >= c), same & (r > c)


def _col(x, j):
    return x[:, j:j + 1]


def _row(x, i):
    return x[i:i + 1, :]


def _params(sem):
    return pltpu.CompilerParams(dimension_semantics=sem, vmem_limit_bytes=VMEM_LIMIT)


def _proj_body(x_ref, g_ref, w_ref, o_ref, h_ref):
    @pl.when(pl.program_id(1) == 0)
    def _():
        h_ref[...] = _bf(_rms(x_ref[...], g_ref[...]))
    o_ref[...] = jnp.dot(h_ref[...], w_ref[...], preferred_element_type=F32)


def _proj(x2, g, w):
    n = x2.shape[0]
    tm = min(1024, n)
    tn = N_PROJ // 4
    return pl.pallas_call(
        _proj_body,
        grid=(n // tm, N_PROJ // tn),
        in_specs=[pl.BlockSpec((tm, D_MODEL), lambda i, j: (i, 0)),
                  pl.BlockSpec((1, D_MODEL), lambda i, j: (0, 0)),
                  pl.BlockSpec((D_MODEL, tn), lambda i, j: (0, j))],
        out_specs=pl.BlockSpec((tm, tn), lambda i, j: (i, j)),
        out_shape=jax.ShapeDtypeStruct((n, N_PROJ), F32),
        scratch_shapes=[pltpu.VMEM((tm, D_MODEL), BF16)],
        compiler_params=_params(("arbitrary", "arbitrary")),
        name="in_proj",
    )(x2, g, w)


def _mlstm_body(p_ref, gt_ref, gb_ref, ng_ref, c0_ref, n0_ref, m0_ref,
                y_ref, c_ref, n_ref, m_ref, *, bb, L):
    @pl.when(pl.program_id(1) == 0)
    def _():
        c_ref[...] = c0_ref[...]
        n_ref[...] = n0_ref[...]
        m_ref[...] = m0_ref[...]

    incl, _ = _tri_masks(L)
    tri_lo = _ones_where(incl)
    tri_up = _ones_where(_iota2((L, L), 0) <= _iota2((L, L), 1))
    sel = _ones_where(_iota2((8, LANES), 0) == _iota2((8, LANES), 1))
    lane = _iota2((1, LANES), 1)
    scale = M_DH ** -0.5

    heads, m_updates = [], []
    for b in range(bb):
        gates = gt_ref[b] + gb_ref[...]
        lf_cols = _log_sigmoid(gates)
        b_cols = _dot_exact_rhs(tri_lo, lf_cols)
        rows = _dot_nt_exact_rhs(sel, jnp.where(lane < GATE_MF, gates, lf_cols))
        b_rows = _dot_exact_lhs(rows, tri_up)
        m_row = m_ref[b]
        m_news = []

        def head(h, b=b, gates=gates, b_cols=b_cols, rows=rows, b_rows=b_rows, m_row=m_row, m_news=m_news):
            sl = slice(h * M_DH, (h + 1) * M_DH)
            q = p_ref[b, :, sl] * scale
            k = p_ref[b, :, W_GRP + h * M_DH:W_GRP + (h + 1) * M_DH]
            v = p_ref[b, :, 2 * W_GRP + h * M_DH:2 * W_GRP + (h + 1) * M_DH]
            o = p_ref[b, :, 3 * W_GRP + h * M_DH:3 * W_GRP + (h + 1) * M_DH]
            li_c = _col(gates, GATE_MI + h)
            b_c = _col(b_cols, GATE_MF + h)
            li_r = _row(rows, GATE_MI + h)
            b_r = _row(b_rows, GATE_MF + h)
            m_prev = jnp.sum(jnp.where(lane == h, m_row, 0.0), axis=1, keepdims=True)
            c_st = c_ref[b, h]
            n_st = n_ref[b, h:h + 1, :]

            log_d = jnp.where(incl, b_c - b_r + li_r, -jnp.inf)
            inter = b_c + m_prev
            mt = jnp.maximum(inter, jnp.max(log_d, axis=-1, keepdims=True))
            yield
            s = _dot_nt(q, k) * jnp.exp(log_d - mt)
            w_c = jnp.exp(inter - mt)
            yield
            num = _dot(s, v) + _dot(q, c_st) * w_c
            den = jnp.sum(s, -1, keepdims=True) + w_c * jnp.sum(q * n_st, -1, keepdims=True)
            hh = num / jnp.maximum(jnp.abs(den), jnp.exp(-mt))
            yield

            b_last = b_c[L - 1:L, :]
            tail = b_last - b_c + li_c
            m_new = jnp.maximum(b_last + m_prev, jnp.max(tail, axis=0, keepdims=True))
            wk = jnp.exp(tail - m_new)
            sc = jnp.exp(b_last + m_prev - m_new)
            c_ref[b, h] = sc * c_st + _dot_tn(k, wk * v)
            yield
            n_ref[b, h:h + 1, :] = sc * n_st + jnp.sum(wk * k, axis=0, keepdims=True)
            m_news.append((h, m_new))
            yield

            mu = jnp.mean(hh, -1, keepdims=True)
            hc = hh - mu
            hn = hc * lax.rsqrt(jnp.mean(hc * hc, -1, keepdims=True) + EPS)
            y_ref[b, :, sl] = _bf(_sigmoid(o) * (hn * ng_ref[:, sl]))

        heads.extend(head(h) for h in range(M_HEADS))
        m_updates.append((b, m_row, m_news))
    _run_interleaved(heads)
    for b, m_row, m_news in m_updates:
        m_out = m_row
        for h, m_new in m_news:
            m_out = jnp.where(lane == h, m_new, m_out)
        m_ref[b] = m_out


def _mlstm(proj3, gate_bias, norm_g, c0, n0, m0):
    B, T, _ = proj3.shape
    L = min(CHUNK, T)
    bb = 2
    st = lambda b, t: (b, 0, 0)
    return pl.pallas_call(
        functools.partial(_mlstm_body, bb=bb, L=L),
        grid=(B // bb, T // L),
        in_specs=[pl.BlockSpec((bb, L, 4 * W_GRP), lambda b, t: (b, t, COL_M // (4 * W_GRP))),
                  pl.BlockSpec((bb, L, LANES), lambda b, t: (b, t, COL_GATE // LANES)),
                  pl.BlockSpec((1, LANES), lambda b, t: (0, 0)),
                  pl.BlockSpec((1, W_GRP), lambda b, t: (0, 0)),
                  pl.BlockSpec((bb, M_HEADS, M_DH, M_DH), lambda b, t: (b, 0, 0, 0)),
                  pl.BlockSpec((bb, M_HEADS, M_DH), st),
                  pl.BlockSpec((bb, 1, LANES), st)],
        out_specs=[pl.BlockSpec((bb, L, W_GRP), lambda b, t: (b, t, 0)),
                   pl.BlockSpec((bb, M_HEADS, M_DH, M_DH), lambda b, t: (b, 0, 0, 0)),
                   pl.BlockSpec((bb, M_HEADS, M_DH), st),
                   pl.BlockSpec((bb, 1, LANES), st)],
        out_shape=[jax.ShapeDtypeStruct((B, T, W_GRP), BF16),
                   jax.ShapeDtypeStruct((B, M_HEADS, M_DH, M_DH), F32),
                   jax.ShapeDtypeStruct((B, M_HEADS, M_DH), F32),
                   jax.ShapeDtypeStruct((B, 1, LANES), F32)],
        compiler_params=_params(("arbitrary", "arbitrary")),
        name="mlstm",
    )(proj3, proj3, gate_bias, norm_g, c0, n0, m0)


def _gdn_body(p_ref, gt_ref, alog_ref, dtb_ref, cw_ref, ng_ref, s0_ref, cv0_ref,
              y_ref, s_ref, cv_ref, xp_ref, *, bb, L, Lb):
    t = pl.program_id(1)
    W3 = 3 * W_GRP
    PAD = 8
    H = G_HEADS

    @pl.when(t == 0)
    def _():
        s_ref[...] = s0_ref[...]
        xp_ref[:, PAD - (CONV_W - 1):PAD, :] = cv0_ref[...]

    incl, _ = _tri_masks(L)
    tri_lo = _ones_where(incl)
    incl_bd, strict_bd = _block_masks(H, L)
    sel = _ones_where(_iota2((8, LANES), 0) == _iota2((8, LANES), 1))
    lane = _iota2((1, LANES), 1)
    neg_a = -jnp.exp(alog_ref[...])
    rows_of = lambda x, h: x[h * L:(h + 1) * L]
    stack = lambda xs: jnp.concatenate(xs, axis=0)

    def chunk(b, c0, acc):
        gates = gt_ref[b, c0:c0 + L, :]
        g_cols = neg_a * _softplus(gates + dtb_ref[...])
        beta_cols = _sigmoid(gates)
        gc_cols = _dot_exact_rhs(tri_lo, g_cols)

        qs, ks, vs = [], [], []
        for h in range(H):
            q = acc[:, h * G_DH:(h + 1) * G_DH]
            k = acc[:, W_GRP + h * G_DH:W_GRP + (h + 1) * G_DH]
            qs.append(q * lax.rsqrt(jnp.sum(q * q, -1, keepdims=True) + 1e-6) * (G_DH ** -0.5))
            ks.append(k * lax.rsqrt(jnp.sum(k * k, -1, keepdims=True) + 1e-6))
            vs.append(acc[:, 2 * W_GRP + h * G_DH:2 * W_GRP + (h + 1) * G_DH])
        yield
        q_s, k_s, v_s = stack(qs), stack(ks), stack(vs)
        g_c = stack([_col(gc_cols, GATE_GA + h) for h in range(H)])
        beta = stack([_col(beta_cols, GATE_GB + h) for h in range(H)])
        g_r = _dot_nt_exact_rhs(sel, jnp.where(lane == 0, g_c, 0.0))[0:1, :]
        s_st = [s_ref[b, h] for h in range(H)]

        diff = g_c - g_r
        eg = jnp.exp(g_c)
        a_mat = beta * _dot_nt(k_s, k_s) * jnp.exp(jnp.where(strict_bd, diff, -jnp.inf))
        yield
        tm = yield from _neumann_tail(-a_mat, L)
        rhs = jnp.concatenate([beta * v_s, (beta * eg) * k_s], axis=1)
        sol = rhs + _dot(tm, rhs)
        yield
        u = sol[:, 0:G_DH] - stack([_dot(rows_of(sol, h)[:, G_DH:2 * G_DH], s_st[h]) for h in range(H)])
        qk = _dot_nt(q_s, k_s) * jnp.exp(jnp.where(incl_bd, diff, -jnp.inf))
        yield
        o_s = eg * stack([_dot(qs[h], s_st[h]) for h in range(H)]) + _dot(qk, u)
        yield

        for h in range(H):
            sl = slice(h * G_DH, (h + 1) * G_DH)
            g_ch = rows_of(g_c, h)
            g_last = g_ch[L - 1:L, :]
            s_ref[b, h] = jnp.exp(g_last) * s_st[h] + _dot_tn(ks[h] * jnp.exp(g_last - g_ch), rows_of(u, h))
            o = rows_of(o_s, h)
            o = o * lax.rsqrt(jnp.mean(o * o, -1, keepdims=True) + EPS) * ng_ref[...]
            z = p_ref[b, c0:c0 + L, W3 + h * G_DH:W3 + (h + 1) * G_DH]
            y_ref[b, c0:c0 + L, sl] = _bf(o * _silu(z))
            yield

    def stream(b, acc):
        for c0 in range(0, Lb, L):
            yield from chunk(b, c0, acc[c0:c0 + L])

    streams = []
    for b in range(bb):
        xp_ref[b, PAD:PAD + Lb, :] = p_ref[b, :, 0:W3]
        acc = xp_ref[b, PAD - 3:PAD - 3 + Lb, :] * cw_ref[0:1, :]
        for j in range(1, CONV_W):
            acc = acc + xp_ref[b, PAD - 3 + j:PAD - 3 + j + Lb, :] * cw_ref[j:j + 1, :]
        tail3 = xp_ref[b, PAD + Lb - (CONV_W - 1):PAD + Lb, :]
        xp_ref[b, PAD - (CONV_W - 1):PAD, :] = tail3
        cv_ref[b] = tail3
        streams.append(stream(b, _silu(acc)))
    _run_interleaved(streams)


def _gdn(proj3, alog_row, dtb_row, conv_w, norm_g, s0, cv0):
    B, T, _ = proj3.shape
    L = min(CHUNK, T)
    Lb = min(CHUNKS_PER_STEP * L, T)
    bb = 2
    return pl.pallas_call(
        functools.partial(_gdn_body, bb=bb, L=L, Lb=Lb),
        grid=(B // bb, T // Lb),
        in_specs=[pl.BlockSpec((bb, Lb, 4 * W_GRP), lambda b, t: (b, t, COL_G // (4 * W_GRP))),
                  pl.BlockSpec((bb, Lb, LANES), lambda b, t: (b, t, COL_GATE // LANES)),
                  pl.BlockSpec((1, LANES), lambda b, t: (0, 0)),
                  pl.BlockSpec((1, LANES), lambda b, t: (0, 0)),
                  pl.BlockSpec((CONV_W, 3 * W_GRP), lambda b, t: (0, 0)),
                  pl.BlockSpec((1, G_DH), lambda b, t: (0, 0)),
                  pl.BlockSpec((bb, G_HEADS, G_DH, G_DH), lambda b, t: (b, 0, 0, 0)),
                  pl.BlockSpec((bb, CONV_W - 1, 3 * W_GRP), lambda b, t: (b, 0, 0))],
        out_specs=[pl.BlockSpec((bb, Lb, W_GRP), lambda b, t: (b, t, 0)),
                   pl.BlockSpec((bb, G_HEADS, G_DH, G_DH), lambda b, t: (b, 0, 0, 0)),
                   pl.BlockSpec((bb, CONV_W - 1, 3 * W_GRP), lambda b, t: (b, 0, 0))],
        out_shape=[jax.ShapeDtypeStruct((B, T, W_GRP), BF16),
                   jax.ShapeDtypeStruct((B, G_HEADS, G_DH, G_DH), F32),
                   jax.ShapeDtypeStruct((B, CONV_W - 1, 3 * W_GRP), F32)],
        scratch_shapes=[pltpu.VMEM((bb, 8 + Lb, 3 * W_GRP), F32)],
        compiler_params=_params(("arbitrary", "arbitrary")),
        name="gdn",
    )(proj3, proj3, alog_row, dtb_row, conv_w, norm_g, s0, cv0)


def _rwkv_body(p_ref, mu_ref, wwa_ref, w0_ref, a0_ref, g2_ref, kk_ref, ka_ref, rk_ref,
               lnw_ref, lnb_ref, s0_ref, sh0_ref,
               y_ref, s_ref, sh_ref, xs_ref, *, bb, L, Lb):
    t = pl.program_id(1)
    PAD = 8
    NQ = 4

    @pl.when(t == 0)
    def _():
        s_ref[...] = s0_ref[...]
        xs_ref[:, PAD - 1:PAD, :] = sh0_ref[...]

    incl, _ = _tri_masks(L)
    tri_lo = _ones_where(incl)
    incl_bd, strict_bd = _block_masks(NQ, L)
    lane = _iota2((1, LANES), 1)
    head_masks = (lane < R_DH, lane >= R_DH)
    r_i, c_i = _iota2((LANES, LANES), 0), _iota2((LANES, LANES), 1)
    same_head = (r_i < R_DH) == (c_i < R_DH)
    seg = _ones_where(same_head)
    seg_sum = lambda x: _dot_exact_lhs(x, seg)
    rows_of = lambda x, i: x[i * L:(i + 1) * L]
    stack = lambda xs: jnp.concatenate(xs, axis=0)

    groups = []
    for b in range(bb):
        xs_ref[b, PAD:PAD + Lb, :] = p_ref[b, :, 0:N_R]
        cur = p_ref[b, :, 0:N_R]
        prev = xs_ref[b, PAD - 1:PAD - 1 + Lb, :]
        last = xs_ref[b, PAD + Lb - 1:PAD + Lb, :]
        xs_ref[b, PAD - 1:PAD, :] = last
        sh_ref[b] = last

        xm = cur + (prev - cur) * mu_ref[...]
        r_all = xm[:, 0:W_GRP]
        k_all = xm[:, W_GRP:2 * W_GRP]
        v_all = xm[:, 2 * W_GRP:3 * W_GRP]
        wa_code = xm[:, 3 * W_GRP:3 * W_GRP + LANES]
        g_code = xm[:, 3 * W_GRP + LANES:3 * W_GRP + 2 * LANES]
        wa_in = jnp.where(lane < R_LR_W, jnp.tanh(wa_code), wa_code)
        wa = _dot(wa_in, wwa_ref[...])
        w_all = -_softplus(-(w0_ref[...] + wa[:, 0:W_GRP])) - 0.5
        ld_all = -jnp.exp(w_all)
        a_all = _sigmoid(a0_ref[...] + wa[:, W_GRP:2 * W_GRP])
        g_all = _dot(_sigmoid(g_code), g2_ref[...])
        k2_all = k_all * (1.0 + (a_all - 1.0) * ka_ref[...])
        kkr_all = k_all * kk_ref[...]

        def quad(qd, c0, b=b, r_all=r_all, k2_all=k2_all, v_all=v_all, ld_all=ld_all, a_all=a_all,
                 kkr_all=kkr_all, g_all=g_all):
            rs = slice(c0, c0 + L)
            pairs = (2 * qd, 2 * qd + 1)
            pr = {}
            for p in pairs:
                sl = slice(p * LANES, (p + 1) * LANES)
                r, k, v, ld, a, kkr = (r_all[rs, sl], k2_all[rs, sl], v_all[rs, sl], ld_all[rs, sl], a_all[rs, sl],
                                       kkr_all[rs, sl])
                kk = kkr * lax.rsqrt(seg_sum(kkr * kkr) + 1e-6)
                cum = _dot_exact_rhs(tri_lo, ld)
                e_neg = jnp.exp(-cum)
                pr[p] = dict(r=r, k=k, v=v, cum=cum, a_t=jnp.exp(cum - ld) * kk, b_t=-(kk * a) * e_neg,
                             k_t=k * e_neg, r_t=r * jnp.exp(cum), s=s_ref[b, p])
                yield
            slots = [(p, hm) for p in pairs for hm in head_masks]
            a_s = stack([jnp.where(hm, pr[p]['a_t'], 0.0) for p, hm in slots])
            r_s = stack([jnp.where(hm, pr[p]['r_t'], 0.0) for p, hm in slots])
            b_s = stack([pr[p]['b_t'] for p, _ in slots])
            k_s = stack([pr[p]['k_t'] for p, _ in slots])
            v_s = stack([pr[p]['v'] for p, _ in slots])
            a_ab = jnp.where(strict_bd, _dot_nt(a_s, b_s), 0.0)
            a_ak = jnp.where(strict_bd, _dot_nt(a_s, k_s), 0.0)
            yield
            a_rb = jnp.where(incl_bd, _dot_nt(r_s, b_s), 0.0)
            a_rk = jnp.where(incl_bd, _dot_nt(r_s, k_s), 0.0)
            yield
            tm = yield from _neumann_tail(a_ab, L)
            rhs = stack([_dot_nt(rows_of(a_s, i), pr[p]['s']) for i, (p, _) in enumerate(slots)]) + _dot(a_ak, v_s)
            yield
            u_s = rhs + _dot(tm, rhs)
            yield
            y_s = _dot(a_rb, u_s) + _dot(a_rk, v_s)
            yield

            for j, p in enumerate(pairs):
                sl = slice(p * LANES, (p + 1) * LANES)
                d = pr[p]
                u = jnp.where(head_masks[0], rows_of(u_s, 2 * j), rows_of(u_s, 2 * j + 1))
                y = _dot_nt(d['r_t'], d['s']) + jnp.where(head_masks[0], rows_of(y_s, 2 * j), rows_of(y_s, 2 * j + 1))
                w_last = jnp.exp(d['cum'][L - 1:L, :])
                s_new = (d['s'] + _dot_tn(u, d['b_t']) + _dot_tn(d['v'], d['k_t'])) * w_last
                s_ref[b, p] = jnp.where(same_head, s_new, 0.0)

                mean = seg_sum(y) * (1.0 / R_DH)
                yc = y - mean
                var = seg_sum(yc * yc) * (1.0 / R_DH)
                yn = yc * lax.rsqrt(var + GN_EPS) * lnw_ref[:, sl] + lnb_ref[:, sl]
                bonus = seg_sum(d['r'] * d['k'] * rk_ref[:, sl]) * d['v']
                y_ref[b, rs, sl] = _bf((yn + bonus) * g_all[rs, sl])
                yield

        def group(qd, quad=quad):
            for c0 in range(0, Lb, L):
                yield from quad(qd, c0)

        groups.extend(group(qd) for qd in range(R_HEADS // NQ))
    _run_interleaved(groups)


def _rwkv(proj3, mu, wwa, w0, a0, g2, k_k, k_a, r_k, ln_w, ln_b, s0, sh0):
    B, T, _ = proj3.shape
    L = min(CHUNK, T)
    Lb = min(CHUNKS_PER_STEP * L, T)
    bb = 2
    NP = R_HEADS // 2
    vec = lambda n: pl.BlockSpec((1, n), lambda b, t: (0, 0))
    return pl.pallas_call(
        functools.partial(_rwkv_body, bb=bb, L=L, Lb=Lb),
        grid=(B // bb, T // Lb),
        in_specs=[pl.BlockSpec((bb, Lb, 4 * W_GRP), lambda b, t: (b, t, COL_R // (4 * W_GRP))),
                  vec(N_R),
                  pl.BlockSpec((LANES, 2 * W_GRP), lambda b, t: (0, 0)),
                  vec(W_GRP), vec(W_GRP),
                  pl.BlockSpec((R_LR_G, W_GRP), lambda b, t: (0, 0)),
                  vec(W_GRP), vec(W_GRP), vec(W_GRP), vec(W_GRP), vec(W_GRP),
                  pl.BlockSpec((bb, NP, LANES, LANES), lambda b, t: (b, 0, 0, 0)),
                  pl.BlockSpec((bb, 1, N_R), lambda b, t: (b, 0, 0))],
        out_specs=[pl.BlockSpec((bb, Lb, W_GRP), lambda b, t: (b, t, 0)),
                   pl.BlockSpec((bb, NP, LANES, LANES), lambda b, t: (b, 0, 0, 0)),
                   pl.BlockSpec((bb, 1, N_R), lambda b, t: (b, 0, 0))],
        out_shape=[jax.ShapeDtypeStruct((B, T, W_GRP), BF16),
                   jax.ShapeDtypeStruct((B, NP, LANES, LANES), F32),
                   jax.ShapeDtypeStruct((B, 1, N_R), F32)],
        scratch_shapes=[pltpu.VMEM((bb, 8 + Lb, N_R), F32)],
        compiler_params=_params(("arbitrary", "arbitrary")),
        name="rwkv7",
    )(proj3, mu, wwa, w0, a0, g2, k_k, k_a, r_k, ln_w, ln_b, s0, sh0)


def _s5_body(u_ref, win_ref, wout_ref, lam_ref, pw_ref, d_ref, wglu_ref, bglu_ref, hr0_ref, hi0_ref,
             y_ref, hr_ref, hi_ref, sr_ref, si_ref, *, Lb):
    @pl.when(pl.program_id(1) == 0)
    def _():
        hr_ref[...] = hr0_ref[...]
        hi_ref[...] = hi0_ref[...]

    NB = S5_N // W_GRP
    u = u_ref[0]
    for c in range(NB):
        bu = _dot(u[:, c * LANES:(c + 1) * LANES], win_ref[c])
        sr_ref[:, c * W_GRP:(c + 1) * W_GRP] = bu[:, 0:W_GRP]
        si_ref[:, c * W_GRP:(c + 1) * W_GRP] = bu[:, W_GRP:2 * W_GRP]

    row8 = _iota2((8, S5_N), 0)
    lam_r = [lam_ref[k:k + 1, :] for k in range(3)]
    lam_i = [lam_ref[3 + k:4 + k, :] for k in range(3)]
    pw_r = pw_ref[0:8, :]
    pw_i = pw_ref[8:16, :]

    def tile(n, carry):
        cr, ci = carry
        base = pl.multiple_of(n * 8, 8)
        xr = sr_ref[pl.ds(base, 8), :]
        xi = si_ref[pl.ds(base, 8), :]
        for lvl, sft in enumerate((1, 2, 4)):
            keep = row8 >= sft
            zr = jnp.where(keep, pltpu.roll(xr, sft, axis=0), 0.0)
            zi = jnp.where(keep, pltpu.roll(xi, sft, axis=0), 0.0)
            xr, xi = (xr + lam_r[lvl] * zr - lam_i[lvl] * zi,
                      xi + lam_r[lvl] * zi + lam_i[lvl] * zr)
        xr, xi = xr + pw_r * cr - pw_i * ci, xi + pw_r * ci + pw_i * cr
        sr_ref[pl.ds(base, 8), :] = xr
        si_ref[pl.ds(base, 8), :] = xi
        return xr[7:8, :], xi[7:8, :]

    cr, ci = lax.fori_loop(0, Lb // 8, tile, (hr_ref[0], hi_ref[0]))
    hr_ref[0] = cr
    hi_ref[0] = ci

    ys = []
    for c in range(NB):
        sl = slice(c * W_GRP, (c + 1) * W_GRP)
        ys.append(_dot(sr_ref[:, sl], wout_ref[c, 0]) + _dot(si_ref[:, sl], wout_ref[c, 1]))
    y = jnp.concatenate(ys, axis=-1) + d_ref[...] * u
    y = _gelu_tanh(y)
    y_ref[0] = _bf(y * _sigmoid(_dot(y, wglu_ref[...]) + bglu_ref[...]))


def _s5(proj3, win, wout, lam_pows, row_pows, d_vec, w_glu, b_glu, hr0, hi0):
    B, T, _ = proj3.shape
    Lb = min(512, T)
    st = pl.BlockSpec((1, 1, S5_N), lambda b, t: (b, 0, 0))
    full = lambda a: pl.BlockSpec(a.shape, lambda b, t: (0,) * a.ndim)
    return pl.pallas_call(
        functools.partial(_s5_body, Lb=Lb),
        grid=(B, T // Lb),
        in_specs=[pl.BlockSpec((1, Lb, W_GRP), lambda b, t: (b, t, COL_S // W_GRP)),
                  full(win), full(wout), full(lam_pows), full(row_pows), full(d_vec),
                  full(w_glu), full(b_glu), st, st],
        out_specs=[pl.BlockSpec((1, Lb, W_GRP), lambda b, t: (b, t, 0)), st, st],
        out_shape=[jax.ShapeDtypeStruct((B, T, W_GRP), BF16),
                   jax.ShapeDtypeStruct((B, 1, S5_N), F32),
                   jax.ShapeDtypeStruct((B, 1, S5_N), F32)],
        scratch_shapes=[pltpu.VMEM((Lb, S5_N), F32), pltpu.VMEM((Lb, S5_N), F32)],
        compiler_params=_params(("arbitrary", "arbitrary")),
        name="s5",
    )(proj3, win, wout, lam_pows, row_pows, d_vec, w_glu, b_glu, hr0, hi0)


def _ffn_body(x_ref, ym_ref, ys_ref, yr_ref, yg_ref, wo_ref, gpm_ref, gpf_ref, wg_ref, wu_ref, wd_ref, gpo_ref,
              o_ref, hf_ref, acc_ref):
    f = pl.program_id(1)

    @pl.when(f == 0)
    def _():
        mix = jnp.dot(ym_ref[...], wo_ref[0], preferred_element_type=F32)
        mix = mix + jnp.dot(ys_ref[...], wo_ref[1], preferred_element_type=F32)
        mix = mix + jnp.dot(yr_ref[...], wo_ref[2], preferred_element_type=F32)
        mix = mix + jnp.dot(yg_ref[...], wo_ref[3], preferred_element_type=F32)
        x1 = x_ref[...] + _rms(mix, gpm_ref[...])
        o_ref[...] = x1
        hf_ref[...] = _bf(_rms(x1, gpf_ref[...]))
        acc_ref[...] = jnp.zeros_like(acc_ref)

    hf = hf_ref[...]
    gate = jnp.dot(hf, wg_ref[...], preferred_element_type=F32)
    up = jnp.dot(hf, wu_ref[...], preferred_element_type=F32)
    acc_ref[...] += jnp.dot(_bf(_silu(gate) * up), wd_ref[...], preferred_element_type=F32)

    @pl.when(f == pl.num_programs(1) - 1)
    def _():
        o_ref[...] = o_ref[...] + _rms(acc_ref[...], gpo_ref[...])


def _out_ffn(x2, ym, ys, yr, yg, w_out, g_post_mix, g_pre_ffn, w_gate, w_up, w_down, g_post_ffn):
    n = x2.shape[0]
    tm = min(512, n)
    tf = 512
    row = lambda w: pl.BlockSpec((tm, w), lambda i, f: (i, 0))
    vec = pl.BlockSpec((1, D_MODEL), lambda i, f: (0, 0))
    return pl.pallas_call(
        _ffn_body,
        grid=(n // tm, D_FF // tf),
        in_specs=[row(D_MODEL), row(W_GRP), row(W_GRP), row(W_GRP), row(W_GRP),
                  pl.BlockSpec((4, W_GRP, D_MODEL), lambda i, f: (0, 0, 0), pipeline_mode=pl.Buffered(1)),
                  vec, vec,
                  pl.BlockSpec((D_MODEL, tf), lambda i, f: (0, f)),
                  pl.BlockSpec((D_MODEL, tf), lambda i, f: (0, f)),
                  pl.BlockSpec((tf, D_MODEL), lambda i, f: (f, 0)),
                  vec],
        out_specs=row(D_MODEL),
        out_shape=jax.ShapeDtypeStruct((n, D_MODEL), F32),
        scratch_shapes=[pltpu.VMEM((tm, D_MODEL), BF16), pltpu.VMEM((tm, D_MODEL), F32)],
        compiler_params=_params(("arbitrary", "arbitrary")),
        name="out_ffn",
    )(x2, ym, ys, yr, yg, w_out, g_post_mix, g_pre_ffn, w_gate, w_up, w_down, g_post_ffn)


def _lane_row(pairs):
    row = jnp.zeros((LANES,), F32)
    for off, vals in pairs:
        row = lax.dynamic_update_slice(row, vals.astype(F32), (off,))
    return row[None, :]


def _block_diag(blocks):
    n, r, c = blocks.shape
    eye = jnp.eye(n, dtype=blocks.dtype)
    return (eye[:, None, :, None] * blocks[:, :, None, :]).reshape(n * r, n * c)


def _complex_pow(re, im, n):
    pr, pi = re, im
    for _ in range(n - 1):
        pr, pi = pr * re - pi * im, pr * im + pi * re
    return pr, pi


def _prep_layer(p):
    f32 = lambda a: a.astype(F32)
    w_in = p['w_in']
    off_s = N_M
    off_r = N_M + N_S
    off_g = N_M + N_S + N_R
    gate_cols = jnp.concatenate([w_in[:, 4 * W_GRP:N_M], w_in[:, off_g + 4 * W_GRP:off_g + N_G]], axis=1)
    w_perm = jnp.concatenate([
        w_in[:, 0:4 * W_GRP],
        w_in[:, off_g:off_g + 4 * W_GRP],
        w_in[:, off_r:off_r + N_R],
        gate_cols,
        jnp.zeros((D_MODEL, COL_S - COL_GATE - 16), w_in.dtype),
        w_in[:, off_s:off_s + N_S]], axis=1)
    q = {'w_in': _bf(w_perm)}
    q['g_pre_mix'] = f32(p['g_pre_mix'])[None, :]

    gb = f32(p['mlstm_gate_bias'])
    q['m_gate_bias'] = _lane_row([(GATE_MI, gb[0]), (GATE_MF, gb[1])])
    q['m_norm_g'] = f32(p['mlstm_norm_g'])[None, :]

    lam_re, lam_im = f32(p['s5_lam_re']), f32(p['s5_lam_im'])
    dt = jnp.exp(f32(p['s5_log_dt']))[:, None]
    mag = jnp.exp(lam_re * dt)
    lb_re = mag * jnp.cos(lam_im * dt)
    lb_im = mag * jnp.sin(lam_im * dt)
    nr = lb_re - 1.0
    den = lam_re * lam_re + lam_im * lam_im
    f_re = (nr * lam_re + lb_im * lam_im) / den
    f_im = (lb_im * lam_re - nr * lam_im) / den
    B_re, B_im = f32(p['s5_B_re']), f32(p['s5_B_im'])
    Bb_re = f_re[..., None] * B_re - f_im[..., None] * B_im
    Bb_im = f_re[..., None] * B_im + f_im[..., None] * B_re
    nb, gpb = S5_N // W_GRP, S5_GROUPS // (S5_N // W_GRP)
    bd_in = lambda m: jnp.stack([_block_diag(jnp.swapaxes(m, 1, 2)[c * gpb:(c + 1) * gpb]) for c in range(nb)])
    q['s5_win'] = _bf(jnp.concatenate([bd_in(Bb_re), bd_in(Bb_im)], axis=-1))
    bd_out = lambda m: jnp.stack([_block_diag(jnp.swapaxes(m, 1, 2)[c * gpb:(c + 1) * gpb]) for c in range(nb)])
    q['s5_wout'] = _bf(jnp.stack([bd_out(f32(p['s5_C_re'])), -bd_out(f32(p['s5_C_im']))], axis=1))
    lr, li = lb_re.reshape(1, S5_N), lb_im.reshape(1, S5_N)
    pows = [_complex_pow(lr, li, n) for n in range(1, 9)]
    zero = jnp.zeros((2, S5_N), F32)
    q['s5_lam_pows'] = jnp.concatenate([pows[0][0], pows[1][0], pows[3][0], pows[0][1], pows[1][1], pows[3][1], zero], 0)
    q['s5_row_pows'] = jnp.concatenate([pw[0] for pw in pows] + [pw[1] for pw in pows], 0)
    q['s5_D'] = f32(p['s5_D'])[None, :]
    q['s5_w_glu'] = _bf(p['s5_w_glu'])
    q['s5_b_glu'] = f32(p['s5_b_glu'])[None, :]

    q['r_mu'] = f32(p['rwkv_mu'])[None, :]
    zw = jnp.zeros((R_LR_W, W_GRP), F32)
    q['r_wwa'] = _bf(jnp.concatenate([jnp.concatenate([f32(p['rwkv_w2']), zw], 1),
                                      jnp.concatenate([zw, f32(p['rwkv_a2'])], 1)], 0))
    q['r_w0'] = f32(p['rwkv_w0'])[None, :]
    q['r_a0'] = f32(p['rwkv_a0'])[None, :]
    q['r_g2'] = _bf(p['rwkv_g2'])
    q['r_k_k'] = f32(p['rwkv_k_k'])[None, :]
    q['r_k_a'] = f32(p['rwkv_k_a'])[None, :]
    q['r_r_k'] = f32(p['rwkv_r_k']).reshape(1, W_GRP)
    q['r_ln_w'] = f32(p['rwkv_ln_w'])[None, :]
    q['r_ln_b'] = f32(p['rwkv_ln_b'])[None, :]

    q['g_alog'] = _lane_row([(GATE_GA, f32(p['gdn_A_log']))])
    q['g_dtb'] = _lane_row([(GATE_GA, f32(p['gdn_dt_bias']))])
    q['g_conv_w'] = f32(p['gdn_conv_w'])
    q['g_norm_g'] = f32(p['gdn_norm_g'])[None, :]

    q['w_out'] = _bf(p['w_out']).reshape(4, W_GRP, D_MODEL)
    q['g_post_mix'] = f32(p['g_post_mix'])[None, :]
    q['g_pre_ffn'] = f32(p['g_pre_ffn'])[None, :]
    q['w_gate'] = _bf(p['w_gate'])
    q['w_up'] = _bf(p['w_up'])
    q['w_down'] = _bf(p['w_down'])
    q['g_post_ffn'] = f32(p['g_post_ffn'])[None, :]
    return q


def _pack_rwkv_state(s):
    B = s.shape[0]
    s = s.reshape(B, R_HEADS // 2, 2, R_DH, R_DH)
    z = jnp.zeros_like(s[:, :, 0])
    top = jnp.concatenate([s[:, :, 0], z], axis=-1)
    bot = jnp.concatenate([z, s[:, :, 1]], axis=-1)
    return jnp.concatenate([top, bot], axis=-2)


def _unpack_rwkv_state(s):
    B = s.shape[0]
    a = s[:, :, :R_DH, :R_DH]
    d = s[:, :, R_DH:, R_DH:]
    return jnp.stack([a, d], axis=2).reshape(B, R_HEADS, R_DH, R_DH)


def _layer(x, q, state):
    mC0, mn0, mm0, s5r0, s5i0, rS0, rsh0, gS0, gcv0 = state
    B, T, _ = x.shape
    x2 = x.reshape(B * T, D_MODEL)
    proj3 = _proj(x2, q['g_pre_mix'], q['w_in']).reshape(B, T, N_PROJ)

    mm0p = jnp.pad(mm0, ((0, 0), (0, LANES - M_HEADS)))[:, None, :]
    y_m, mC, mn, mmp = _mlstm(proj3, q['m_gate_bias'], q['m_norm_g'], mC0, mn0, mm0p)
    mm = mmp[:, 0, :M_HEADS]

    y_s, s5r, s5i = _s5(proj3, q['s5_win'], q['s5_wout'], q['s5_lam_pows'], q['s5_row_pows'], q['s5_D'],
                        q['s5_w_glu'], q['s5_b_glu'],
                        s5r0.reshape(B, 1, S5_N), s5i0.reshape(B, 1, S5_N))
    s5r = s5r.reshape(B, S5_GROUPS, S5_P)
    s5i = s5i.reshape(B, S5_GROUPS, S5_P)

    y_r, rSp, rsh = _rwkv(proj3, q['r_mu'], q['r_wwa'], q['r_w0'], q['r_a0'], q['r_g2'], q['r_k_k'], q['r_k_a'],
                          q['r_r_k'], q['r_ln_w'], q['r_ln_b'], _pack_rwkv_state(rS0), rsh0[:, None, :])
    rS = _unpack_rwkv_state(rSp)
    rsh = rsh[:, 0, :]

    y_g, gS, gcv = _gdn(proj3, q['g_alog'], q['g_dtb'], q['g_conv_w'], q['g_norm_g'], gS0, gcv0)

    w = lambda a: a.reshape(B * T, W_GRP)
    out = _out_ffn(x2, w(y_m), w(y_s), w(y_r), w(y_g), q['w_out'], q['g_post_mix'], q['g_pre_ffn'],
                   q['w_gate'], q['w_up'], q['w_down'], q['g_post_ffn'])
    return out.reshape(B, T, D_MODEL), (mC, mn, mm, s5r, s5i, rS, rsh, gS, gcv)


def _zero_state(b):
    z = lambda *s: jnp.zeros(s, F32)
    return (z(b, M_HEADS, M_DH, M_DH), z(b, M_HEADS, M_DH), z(b, M_HEADS),
            z(b, S5_GROUPS, S5_P), z(b, S5_GROUPS, S5_P),
            z(b, R_HEADS, R_DH, R_DH), z(b, N_R),
            z(b, G_HEADS, G_DH, G_DH), z(b, CONV_W - 1, 3 * W_GRP))


_PARAM_NAMES = ('g_pre_mix', 'w_in', 'mlstm_gate_bias', 'mlstm_norm_g',
                's5_lam_re', 's5_lam_im', 's5_log_dt', 's5_B_re', 's5_B_im', 's5_C_re', 's5_C_im',
                's5_D', 's5_w_glu', 's5_b_glu',
                'rwkv_mu', 'rwkv_w0', 'rwkv_w2', 'rwkv_a0', 'rwkv_a2', 'rwkv_g2', 'rwkv_k_k', 'rwkv_k_a',
                'rwkv_r_k', 'rwkv_ln_w', 'rwkv_ln_b',
                'gdn_conv_w', 'gdn_A_log', 'gdn_dt_bias', 'gdn_norm_g',
                'w_out', 'g_post_mix', 'g_pre_ffn', 'w_gate', 'w_up', 'w_down', 'g_post_ffn')


def kernel(x_prompt, x_sample, state_mlstm_C, state_mlstm_n, state_mlstm_m, state_s5_re, state_s5_im, state_rwkv_S, state_rwkv_shift, state_gdn_S, state_gdn_conv, g_pre_mix, w_in, mlstm_gate_bias, mlstm_norm_g, s5_lam_re, s5_lam_im, s5_log_dt, s5_B_re, s5_B_im, s5_C_re, s5_C_im, s5_D, s5_w_glu, s5_b_glu, rwkv_mu, rwkv_w0, rwkv_w2, rwkv_a0, rwkv_a2, rwkv_g2, rwkv_k_k, rwkv_k_a, rwkv_r_k, rwkv_ln_w, rwkv_ln_b, gdn_conv_w, gdn_A_log, gdn_dt_bias, gdn_norm_g, w_out, g_post_mix, g_pre_ffn, w_gate, w_up, w_down, g_post_ffn):
    weights = (g_pre_mix, w_in, mlstm_gate_bias, mlstm_norm_g,
               s5_lam_re, s5_lam_im, s5_log_dt, s5_B_re, s5_B_im, s5_C_re, s5_C_im, s5_D, s5_w_glu, s5_b_glu,
               rwkv_mu, rwkv_w0, rwkv_w2, rwkv_a0, rwkv_a2, rwkv_g2, rwkv_k_k, rwkv_k_a, rwkv_r_k,
               rwkv_ln_w, rwkv_ln_b, gdn_conv_w, gdn_A_log, gdn_dt_bias, gdn_norm_g,
               w_out, g_post_mix, g_pre_ffn, w_gate, w_up, w_down, g_post_ffn)
    caches = (state_mlstm_C, state_mlstm_n, state_mlstm_m, state_s5_re, state_s5_im,
              state_rwkv_S, state_rwkv_shift, state_gdn_S, state_gdn_conv)
    yp, ys = x_prompt, x_sample
    outs_p, outs_s = [], []
    for l in range(DEPTH):
        q = _prep_layer({name: wt[l] for name, wt in zip(_PARAM_NAMES, weights)})
        yp, st_p = _layer(yp, q, _zero_state(x_prompt.shape[0]))
        ys, st_s = _layer(ys, q, tuple(c[l].astype(F32) for c in caches))
        outs_p.append(st_p)
        outs_s.append(st_s)
    stack = lambda outs: [jnp.stack(t) for t in zip(*outs)]
    return (yp, ys, *stack(outs_p), *stack(outs_s))
```
